```python
import jax, jax.numpy as jnp
from jax import lax
import numpy as np

D_MODEL = 1024
BATCH = 8
SEQ = 8192
DEPTH = 4

HEAD_DIM = 64
N_FOX_HEADS = D_MODEL // (2 * HEAD_DIM)
N_RWKV_HEADS = D_MODEL // (2 * HEAD_DIM)
FOX_WIDTH = N_FOX_HEADS * HEAD_DIM
RWKV_WIDTH = N_RWKV_HEADS * HEAD_DIM
DECAY_LORA = 64
ICLR_LORA = 64
GATE_LORA = 128
RWKV_IN = 3 * RWKV_WIDTH + DECAY_LORA + ICLR_LORA + GATE_LORA
FOX_IN = 3 * FOX_WIDTH + N_FOX_HEADS
EVEN_IN = FOX_IN + RWKV_IN
RWKV_SPLITS = [RWKV_WIDTH, 2 * RWKV_WIDTH, 3 * RWKV_WIDTH,
               3 * RWKV_WIDTH + DECAY_LORA, 3 * RWKV_WIDTH + DECAY_LORA + ICLR_LORA]
N_DIL_HEADS = D_MODEL // HEAD_DIM
DIL_WIDTH = N_DIL_HEADS * HEAD_DIM
ODD_IN = 3 * DIL_WIDTH
DILATED_GROUPS = ((128, 1), (512, 4), (2048, 16))
ROPE_THETA = 500000.0
ROPE_DIMS = HEAD_DIM // 4
D_FF = ((8 * D_MODEL // 3 + 127) // 128) * 128
CONV_WIDTH = 3
Q_BLOCK = 128
RMS_EPS = 1e-6
GN_EPS = 64e-5
NEG_INF = -1e30
N_EVEN = (DEPTH + 1) // 2
N_ODD = DEPTH // 2

kernel_name = "fox_rwkv7_dilated_hybrid"


def rms_norm(x, g):
    xf = x.astype(jnp.float32)
    y = xf * lax.rsqrt(jnp.mean(xf * xf, axis=-1, keepdims=True) + RMS_EPS)
    return (y * g.astype(jnp.float32)).astype(x.dtype)


def shift_right(t):
    return jnp.pad(t, ((0, 0), (1, 0), (0, 0)))[:, :-1]


def rotary_tables(positions):
    inv_freq = ROPE_THETA ** (-jnp.arange(0, ROPE_DIMS, 2, dtype=jnp.float32) / ROPE_DIMS)
    ang = positions.astype(jnp.float32)[..., None] * inv_freq
    return jnp.cos(ang)[:, :, None, :], jnp.sin(ang)[:, :, None, :]


def partial_rotary(t, cos, sin):
    half = ROPE_DIMS // 2
    t1, t2 = t[..., :half], t[..., half:ROPE_DIMS]
    rot = jnp.concatenate([t1 * cos - t2 * sin, t1 * sin + t2 * cos], axis=-1).astype(t.dtype)
    return jnp.concatenate([rot, t[..., ROPE_DIMS:]], axis=-1)


def forgetting_attention(q, k, v, log_f):
    B, T, H, Dh = q.shape
    nb = T // Q_BLOCK
    scale = Dh ** -0.5
    c = jnp.cumsum(log_f, axis=1)
    c_k = c.transpose(0, 2, 1)[:, :, None, :]
    k_pos = jnp.arange(T)
    qb = q.reshape(B, nb, Q_BLOCK, H, Dh).transpose(1, 0, 2, 3, 4)
    cb = c.reshape(B, nb, Q_BLOCK, H).transpose(1, 0, 2, 3)

    def block(args):
        q_blk, c_blk, i = args
        s = jnp.einsum('bqhd,bkhd->bhqk', q_blk, k, preferred_element_type=jnp.float32) * scale
        s = s + c_blk.transpose(0, 2, 1)[..., None] - c_k
        q_pos = i * Q_BLOCK + jnp.arange(Q_BLOCK)
        s = jnp.where(q_pos[:, None] >= k_pos[None, :], s, NEG_INF)
        p = jax.nn.softmax(s, axis=-1)
        return jnp.einsum('bhqk,bkhd->bqhd', p.astype(v.dtype), v)

    o = lax.map(block, (qb, cb, jnp.arange(nb)))
    return o.transpose(1, 0, 2, 3, 4).reshape(B, T, H, Dh)


def rwkv7_scan(r, w, k, v, a, b):
    B, T, H, N = r.shape

    def step(S, inp):
        r_t, w_t, k_t, v_t, a_t, b_t = inp
        sa = jnp.einsum('bhvk,bhk->bhv', S, a_t)
        S = S * w_t[:, :, None, :] + sa[..., None] * b_t[:, :, None, :] + v_t[..., None] * k_t[:, :, None, :]
        return S, jnp.einsum('bhvk,bhk->bhv', S, r_t)

    xs = tuple(t.transpose(1, 0, 2, 3) for t in (r, w, k, v, a, b))
    _, y = lax.scan(step, jnp.zeros((B, H, N, N), jnp.float32), xs)
    return y.transpose(1, 0, 2, 3)


def rwkv7_time_mix(z, mu, w0, w2, a0, a2, g2, k_k, k_a, r_k, ln_w, ln_b):
    B, T, _ = z.shape
    H, N = N_RWKV_HEADS, HEAD_DIM
    heads = lambda t: t.reshape(B, T, H, N)
    z = z.astype(jnp.float32)
    z = z + (shift_right(z) - z) * mu
    r, k, v, w_lo, a_lo, g_lo = jnp.split(z, RWKV_SPLITS, axis=-1)
    w = -jax.nn.softplus(-(w0 + jnp.tanh(w_lo) @ w2)) - 0.5
    decay = jnp.exp(-jnp.exp(w))
    a = jax.nn.sigmoid(a0 + a_lo @ a2)
    g = jax.nn.sigmoid(g_lo) @ g2
    kk = heads(k * k_k)
    kk = kk / jnp.maximum(jnp.sqrt(jnp.sum(kk * kk, axis=-1, keepdims=True)), 1e-12)
    k = k * (1 + (a - 1) * k_a)
    y = rwkv7_scan(heads(r), heads(decay), heads(k), heads(v), -kk, kk * heads(a))
    mean = jnp.mean(y, axis=-1, keepdims=True)
    var = jnp.mean(jnp.square(y - mean), axis=-1, keepdims=True)
    y = ((y - mean) * lax.rsqrt(var + GN_EPS)).reshape(B, T, RWKV_WIDTH) * ln_w + ln_b
    bonus = jnp.sum(heads(r) * heads(k) * r_k, axis=-1, keepdims=True) * heads(v)
    return (y + bonus.reshape(B, T, RWKV_WIDTH)) * g


def fox_rwkv_mixer(h, w_in, forget_bias, mu, w0, w2, a0, a2, g2, k_k, k_a, r_k, ln_w, ln_b, w_out):
    B, T, _ = h.shape
    z = h @ w_in
    fz, rz = z[..., :FOX_IN], z[..., FOX_IN:]
    q, k, v = [fz[..., i * FOX_WIDTH:(i + 1) * FOX_WIDTH].reshape(B, T, N_FOX_HEADS, HEAD_DIM)
               for i in range(3)]
    log_f = jax.nn.log_sigmoid((fz[..., 3 * FOX_WIDTH:] + forget_bias).astype(jnp.float32))
    y_fox = forgetting_attention(q, k, v, log_f).reshape(B, T, FOX_WIDTH)
    y_rwkv = rwkv7_time_mix(rz, mu, w0, w2, a0, a2, g2, k_k, k_a, r_k, ln_w, ln_b)
    y = jnp.concatenate([y_fox, y_rwkv.astype(y_fox.dtype)], axis=-1)
    return y @ w_out


def dilated_branch(q, k, v, window, dilation):
    B, T, H, Dh = q.shape
    span = window // dilation
    L = T // dilation
    nb = -(-L // span)
    Lp = nb * span
    scale = Dh ** -0.5

    def to_blocks(t):
        t = t.reshape(B, L, dilation, H, Dh).transpose(2, 0, 1, 3, 4)
        t = jnp.pad(t, ((0, 0), (0, 0), (0, Lp - L), (0, 0), (0, 0)))
        return t.reshape(dilation, B, nb, span, H, Dh)

    def band(t):
        prev = jnp.pad(t[:, :, :-1], ((0, 0), (0, 0), (1, 0), (0, 0), (0, 0), (0, 0)))
        return jnp.concatenate([prev, t], axis=3)

    def flat(t):
        t = jnp.swapaxes(t, 1, 2)
        return t.reshape((dilation * nb,) + t.shape[2:])

    qf = flat(to_blocks(q))
    kf = flat(band(to_blocks(k)))
    vf = flat(band(to_blocks(v)))
    blk_n = jnp.tile(jnp.arange(nb), dilation)
    qi = jnp.arange(span)[:, None]
    ki = jnp.arange(2 * span)[None, :]
    delta = qi - ki + span
    band_mask = (delta >= 0) & (delta <= span)

    def one(args):
        qx, kx, vx, n = args
        s = jnp.einsum('bqhd,bkhd->bhqk', qx, kx, preferred_element_type=jnp.float32) * scale
        valid = band_mask & ((ki >= span) | (n > 0))
        s = jnp.where(valid, s, NEG_INF)
        m = jnp.max(s, axis=-1, keepdims=True)
        e = jnp.exp(s - m)
        den = jnp.sum(e, axis=-1, keepdims=True)
        o = jnp.einsum('bhqk,bkhd->bqhd', (e / den).astype(vx.dtype), vx)
        return o, (m + jnp.log(den))[..., 0]

    o, lse = lax.map(one, (qf, kf, vf, blk_n))
    o = o.reshape(dilation, nb, B, span, H, Dh).transpose(2, 1, 3, 0, 4, 5)
    o = o.reshape(B, Lp, dilation, H, Dh)[:, :L].reshape(B, T, H, Dh)
    lse = lse.reshape(dilation, nb, B, H, span).transpose(2, 1, 4, 0, 3)
    lse = lse.reshape(B, Lp, dilation, H)[:, :L].reshape(B, T, H)
    return o, lse


def dilated_mixer(h, w_in, w_out, cos, sin):
    B, T, _ = h.shape
    z = h @ w_in
    q, k, v = [z[..., i * DIL_WIDTH:(i + 1) * DIL_WIDTH].reshape(B, T, N_DIL_HEADS, HEAD_DIM)
               for i in range(3)]
    q = partial_rotary(q, cos, sin)
    k = partial_rotary(k, cos, sin)
    outs, lses = zip(*[dilated_branch(q, k, v, w, d) for (w, d) in DILATED_GROUPS])
    wts = jax.nn.softmax(jnp.stack(lses, axis=0), axis=0)
    o = jnp.sum(wts[..., None] * jnp.stack(outs, axis=0).astype(jnp.float32), axis=0)
    return o.reshape(B, T, DIL_WIDTH).astype(h.dtype) @ w_out


def conv_ffn(h, w_up, conv_w, conv_b, w_down):
    T = h.shape[1]
    u = h @ w_up
    up = jnp.pad(u, ((0, 0), (CONV_WIDTH - 1, 0), (0, 0)))
    acc = conv_b
    for j in range(CONV_WIDTH):
        acc = acc + conv_w[j] * up[:, j:j + T]
    gate, val = jnp.split(acc, 2, axis=-1)
    return (jax.nn.gelu(gate, approximate=True) * val) @ w_down


def setup_inputs(seed: int = 0) -> dict:
    key = jax.random.key(seed)
    ks = iter(jax.random.split(key, 32))
    nrm = lambda shape, scale: jax.random.normal(next(ks), shape, jnp.float32) * scale
    uni = lambda shape, lo, hi: jax.random.uniform(next(ks), shape, jnp.float32, lo, hi)
    x = jax.random.normal(next(ks), (BATCH, SEQ, D_MODEL), jnp.float32)
    offsets = jax.random.randint(next(ks), (BATCH, 1), 0, 4096, dtype=jnp.int32)
    positions = (jnp.arange(SEQ, dtype=jnp.int32)[None, :] + offsets).astype(jnp.int32)
    conv_w = jnp.zeros((DEPTH, CONV_WIDTH, 2 * D_FF), jnp.float32).at[:, -1].set(1.0) \
        + nrm((DEPTH, CONV_WIDTH, 2 * D_FF), 0.1)
    return {
        "x": x,
        "positions": positions,
        "norm_mix_pre": 1.0 + nrm((DEPTH, D_MODEL), 0.02),
        "norm_mix_post": 1.0 + nrm((DEPTH, D_MODEL), 0.02),
        "norm_ffn_pre": 1.0 + nrm((DEPTH, D_MODEL), 0.02),
        "norm_ffn_post": 1.0 + nrm((DEPTH, D_MODEL), 0.02),
        "even_w_in": nrm((N_EVEN, D_MODEL, EVEN_IN), D_MODEL ** -0.5),
        "fox_forget_bias": uni((N_EVEN, N_FOX_HEADS), 2.0, 5.0),
        "rwkv_mu": uni((N_EVEN, RWKV_IN), 0.0, 1.0),
        "rwkv_w0": uni((N_EVEN, RWKV_WIDTH), -6.0, 1.0),
        "rwkv_w2": nrm((N_EVEN, DECAY_LORA, RWKV_WIDTH), 0.5 * DECAY_LORA ** -0.5),
        "rwkv_a0": nrm((N_EVEN, RWKV_WIDTH), 0.1),
        "rwkv_a2": nrm((N_EVEN, ICLR_LORA, RWKV_WIDTH), 0.5 * ICLR_LORA ** -0.5),
        "rwkv_g2": nrm((N_EVEN, GATE_LORA, RWKV_WIDTH), GATE_LORA ** -0.5),
        "rwkv_k_k": 0.85 + nrm((N_EVEN, RWKV_WIDTH), 0.05),
        "rwkv_k_a": 1.0 + nrm((N_EVEN, RWKV_WIDTH), 0.05),
        "rwkv_r_k": nrm((N_EVEN, N_RWKV_HEADS, HEAD_DIM), 0.1),
        "rwkv_ln_w": 1.0 + nrm((N_EVEN, RWKV_WIDTH), 0.02),
        "rwkv_ln_b": nrm((N_EVEN, RWKV_WIDTH), 0.02),
        "even_w_out": nrm((N_EVEN, FOX_WIDTH + RWKV_WIDTH, D_MODEL), (FOX_WIDTH + RWKV_WIDTH) ** -0.5),
        "odd_w_in": nrm((N_ODD, D_MODEL, ODD_IN), D_MODEL ** -0.5),
        "odd_w_out": nrm((N_ODD, DIL_WIDTH, D_MODEL), DIL_WIDTH ** -0.5),
        "ffn_w_up": nrm((DEPTH, D_MODEL, 2 * D_FF), D_MODEL ** -0.5),
        "ffn_conv_w": conv_w,
        "ffn_conv_b": nrm((DEPTH, 2 * D_FF), 0.02),
        "ffn_w_down": nrm((DEPTH, D_FF, D_MODEL), D_FF ** -0.5),
    }


def reference(x, positions, norm_mix_pre, norm_mix_post, norm_ffn_pre, norm_ffn_post,
              even_w_in, fox_forget_bias, rwkv_mu, rwkv_w0, rwkv_w2, rwkv_a0, rwkv_a2, rwkv_g2,
              rwkv_k_k, rwkv_k_a, rwkv_r_k, rwkv_ln_w, rwkv_ln_b, even_w_out,
              odd_w_in, odd_w_out, ffn_w_up, ffn_conv_w, ffn_conv_b, ffn_w_down):
    cos, sin = rotary_tables(positions)
    for layer in range(DEPTH):
        i = layer // 2
        h = rms_norm(x, norm_mix_pre[layer])
        if layer % 2 == 0:
            m = fox_rwkv_mixer(h, even_w_in[i], fox_forget_bias[i], rwkv_mu[i], rwkv_w0[i],
                               rwkv_w2[i], rwkv_a0[i], rwkv_a2[i], rwkv_g2[i], rwkv_k_k[i],
                               rwkv_k_a[i], rwkv_r_k[i], rwkv_ln_w[i], rwkv_ln_b[i], even_w_out[i])
        else:
            m = dilated_mixer(h, odd_w_in[i], odd_w_out[i], cos, sin)
        x = x + rms_norm(m, norm_mix_post[layer]).astype(x.dtype)
        h = rms_norm(x, norm_ffn_pre[layer])
        f = conv_ffn(h, ffn_w_up[layer], ffn_conv_w[layer], ffn_conv_b[layer], ffn_w_down[layer])
        x = x + rms_norm(f, norm_ffn_post[layer]).astype(x.dtype)
    return x
```

```python
import functools

import jax
import jax.numpy as jnp
from jax import lax
from jax.experimental import pallas as pl
from jax.experimental.pallas import tpu as pltpu

F32 = jnp.float32
BF16 = jnp.bfloat16

D_MODEL = 1024
HEAD_DIM = 64
N_FOX_HEADS = 8
FOX_WIDTH = 512
RWKV_WIDTH = 512
DECAY_LORA = 64
ICLR_LORA = 64
GATE_LORA = 128
RWKV_IN = 3 * RWKV_WIDTH + DECAY_LORA + ICLR_LORA + GATE_LORA
FOX_IN = 3 * FOX_WIDTH + N_FOX_HEADS
N_DIL_HEADS = 16
DIL_WIDTH = 1024
DILATED_GROUPS = ((128, 1), (512, 4), (2048, 16))
DIL_SPAN = 128
ROPE_THETA = 500000.0
ROPE_DIMS = 16
D_FF = 2816
RMS_EPS = 1e-6
GN_EPS = 64e-5
NEG_INF = -1e30

LANES = 128
V7X_VMEM_LIMIT = 56 * 1024 * 1024

ROW_TILE = 512
FOX_TILE = 512
RWKV_TILE = 256
RWKV_CHUNK = 64
RWKV_GROUP = 256
DIL_QROWS = 512
FFN_CHUNK = 256


def _dot(a, b):
    return jnp.dot(a, b, preferred_element_type=F32)


def _dot_nt(a, b):
    return lax.dot_general(a, b, (((1,), (1,)), ((), ())), preferred_element_type=F32)


def _dot_tn(a, b):
    return lax.dot_general(a, b, (((0,), (0,)), ((), ())), preferred_element_type=F32)


def _split3(x):
    hi = x.astype(BF16)
    r1 = x - hi.astype(F32)
    mid = r1.astype(BF16)
    lo = (r1 - mid.astype(F32)).astype(BF16)
    return hi, mid, lo


def _dot_exact_lhs(sel, x):
    hi, mid, lo = _split3(x)
    return _dot(sel, hi) + _dot(sel, mid) + _dot(sel, lo)


def _dot_exact_rhs(x, sel):
    hi, mid, lo = _split3(x)
    return _dot(hi, sel) + _dot(mid, sel) + _dot(lo, sel)


def _rms_norm(x, g):
    return x * lax.rsqrt(jnp.mean(x * x, axis=-1, keepdims=True) + RMS_EPS) * g


def _softplus(x):
    return jnp.maximum(x, 0.0) + jnp.log1p(jnp.exp(-jnp.abs(x)))


def _sigmoid(x):
    return 1.0 / (1.0 + jnp.exp(-x))


def _const_spec(shape):
    nd = len(shape)
    return pl.BlockSpec(shape, lambda *_: (0,) * nd, pipeline_mode=pl.Buffered(1))


def _params(*sem):
    return pltpu.CompilerParams(dimension_semantics=sem, vmem_limit_bytes=V7X_VMEM_LIMIT)


EVEN_PACKED = 3 * FOX_WIDTH + LANES + RWKV_IN


def _even_in_kernel(x_ref, g_ref, w_ref, fb_ref, q_ref, k_ref, v_ref, c_ref, rz_ref, carry_ref,
                    *, tiles_per_batch):
    i = pl.program_id(0)
    tm = x_ref.shape[0]
    h = _rms_norm(x_ref[...], g_ref[...]).astype(BF16)
    hs = HEAD_DIM ** -0.5
    q_ref[...] = (_dot(h, w_ref[:, 0:FOX_WIDTH]) * hs).astype(BF16)
    k_ref[...] = _dot(h, w_ref[:, FOX_WIDTH:2 * FOX_WIDTH]).astype(BF16)
    v_ref[...] = _dot(h, w_ref[:, 2 * FOX_WIDTH:3 * FOX_WIDTH]).astype(BF16)
    f0 = 3 * FOX_WIDTH
    f = _dot(h, w_ref[:, f0:f0 + LANES]) + fb_ref[...]
    log_f = -_softplus(-f)

    @pl.when(i % tiles_per_batch == 0)
    def _():
        carry_ref[...] = jnp.zeros_like(carry_ref)

    row = lax.broadcasted_iota(jnp.int32, (tm, tm), 0)
    col = lax.broadcasted_iota(jnp.int32, (tm, tm), 1)
    tri = jnp.where(row >= col, 1.0, 0.0).astype(BF16)
    c = _dot_exact_lhs(tri, log_f) + carry_ref[0:1, :]
    c_ref[...] = c
    carry_ref[...] = jnp.broadcast_to(c[tm - 1:tm, :], carry_ref.shape)
    rz_ref[...] = _dot(h, w_ref[:, f0 + LANES:])


def _even_in(x2, g, w_pack, fb, batch):
    n = x2.shape[0]
    tm = ROW_TILE
    row = lambda w: pl.BlockSpec((tm, w), lambda i: (i, 0))
    return pl.pallas_call(
        functools.partial(_even_in_kernel, tiles_per_batch=n // batch // tm),
        grid=(n // tm,),
        in_specs=[row(D_MODEL), _const_spec((1, D_MODEL)), _const_spec((D_MODEL, EVEN_PACKED)),
                  _const_spec((1, LANES))],
        out_specs=[row(FOX_WIDTH), row(FOX_WIDTH), row(FOX_WIDTH), row(LANES), row(RWKV_IN)],
        out_shape=[jax.ShapeDtypeStruct((n, FOX_WIDTH), BF16)] * 3
        + [jax.ShapeDtypeStruct((n, LANES), F32), jax.ShapeDtypeStruct((n, RWKV_IN), F32)],
        scratch_shapes=[pltpu.VMEM((8, LANES), F32)],
        compiler_params=_params("arbitrary"),
        name="even_in",
    )(x2, g, w_pack, fb)


def _fox_kernel(q_ref, k_ref, v_ref, ccol_ref, crow_ref, o_ref):
    hp = pl.program_id(1)
    qi = pl.program_id(2)
    tq = q_ref.shape[0]
    tk = tq
    q = q_ref[...]
    lane = lax.broadcasted_iota(jnp.int32, (tq, LANES), 1)
    ccol = ccol_ref[...]
    heads = []
    for hh in range(2):
        in_head = (lane >= hh * HEAD_DIM) & (lane < (hh + 1) * HEAD_DIM)
        qm = jnp.where(in_head, q, jnp.zeros_like(q))
        h = hp * 2 + hh
        cq = jnp.sum(jnp.where(lane == h, ccol, 0.0), axis=1, keepdims=True)
        heads.append((qm, cq, h))
    rr = lax.broadcasted_iota(jnp.int32, (tq, tk), 0)
    cc = lax.broadcasted_iota(jnp.int32, (tq, tk), 1)
    causal = rr >= cc

    def step(j, carry, masked):
        k0 = pl.multiple_of(j * tk, tk)
        ks = k_ref[pl.ds(k0, tk), :]
        vs = v_ref[pl.ds(k0, tk), :]
        out = []
        for hh in range(2):
            qm, cq, h = heads[hh]
            m, l, acc = carry[hh]
            ck = crow_ref[0, pl.ds(h, 1), pl.ds(k0, tk)]
            s = _dot_nt(qm, ks) + cq - ck
            if masked:
                s = jnp.where(causal, s, NEG_INF)
            m_new = jnp.maximum(m, jnp.max(s, axis=1, keepdims=True))
            p = jnp.exp(s - m_new)
            alpha = jnp.exp(m - m_new)
            l = alpha * l + jnp.sum(p, axis=1, keepdims=True)
            acc = alpha * acc + _dot(p.astype(BF16), vs)
            out.append((m_new, l, acc))
        return tuple(out)

    init = tuple((jnp.full((tq, 1), NEG_INF, F32), jnp.zeros((tq, 1), F32),
                  jnp.zeros((tq, LANES), F32)) for _ in range(2))
    carry = lax.fori_loop(0, qi, lambda j, c: step(j, c, False), init)
    (_, l0, a0), (_, l1, a1) = step(qi, carry, True)
    o_ref[...] = jnp.where(lane < HEAD_DIM, a0 / l0, a1 / l1).astype(BF16)


def _fox_attention(q, k, v, c_col, c_row, batch):
    n = q.shape[0]
    t = n // batch
    tq = FOX_TILE
    nq = t // tq
    pairs = FOX_WIDTH // LANES
    return pl.pallas_call(
        _fox_kernel,
        grid=(batch, pairs, nq),
        in_specs=[
            pl.BlockSpec((tq, LANES), lambda b, p, i: (b * nq + i, p)),
            pl.BlockSpec((t, LANES), lambda b, p, i: (b, p)),
            pl.BlockSpec((t, LANES), lambda b, p, i: (b, p)),
            pl.BlockSpec((tq, LANES), lambda b, p, i: (b * nq + i, 0)),
            pl.BlockSpec((1, N_FOX_HEADS, t), lambda b, p, i: (b, 0, 0)),
        ],
        out_specs=pl.BlockSpec((tq, LANES), lambda b, p, i: (b * nq + i, p)),
        out_shape=jax.ShapeDtypeStruct((n, FOX_WIDTH), BF16),
        compiler_params=_params("arbitrary", "arbitrary", "arbitrary"),
        name="fox_attention",
    )(q, k, v, c_col, c_row)


def _rwkv_kernel(rz_ref, mu_ref, w0_ref, w2_ref, a0_ref, a2_ref, g2_ref, kk_ref, ka_ref, rk_ref,
                 lnw_ref, lnb_ref, y_ref, state_ref, tail_ref, ybuf_ref):
    tc = rz_ref.shape[0]
    ch = RWKV_CHUNK
    gw = RWKV_GROUP
    n_groups = RWKV_WIDTH // gw

    @pl.when(pl.program_id(1) == 0)
    def _():
        state_ref[...] = jnp.zeros_like(state_ref)
        tail_ref[...] = jnp.zeros_like(tail_ref)

    z = rz_ref[...]
    rows = lax.broadcasted_iota(jnp.int32, z.shape, 0)
    z_prev = jnp.where(rows == 0, tail_ref[7:8, :], pltpu.roll(z, 1, axis=0))
    tail_ref[...] = z[tc - 8:tc, :]
    z = z + (z_prev - z) * mu_ref[...]
    r = z[:, 0:RWKV_WIDTH]
    k = z[:, RWKV_WIDTH:2 * RWKV_WIDTH]
    v = z[:, 2 * RWKV_WIDTH:3 * RWKV_WIDTH]
    lo = z[:, 3 * RWKV_WIDTH:3 * RWKV_WIDTH + LANES]
    g_lo = z[:, 3 * RWKV_WIDTH + LANES:]

    w = w0_ref[...] + _dot(jnp.tanh(lo).astype(BF16), w2_ref[...])
    w = -_softplus(-w) - 0.5
    log_decay = -jnp.exp(w)
    a = _sigmoid(a0_ref[...] + _dot(lo.astype(BF16), a2_ref[...]))
    gate = _dot(_sigmoid(g_lo).astype(BF16), g2_ref[...])

    hr = lax.broadcasted_iota(jnp.int32, (RWKV_WIDTH, RWKV_WIDTH), 0) // HEAD_DIM
    hc = lax.broadcasted_iota(jnp.int32, (RWKV_WIDTH, RWKV_WIDTH), 1) // HEAD_DIM
    head_ones = jnp.where(hr == hc, 1.0, 0.0).astype(BF16)
    seg_sum = lambda t: _dot_exact_rhs(t, head_ones)

    kk = k * kk_ref[...]
    kk = kk / jnp.maximum(jnp.sqrt(seg_sum(kk * kk)), 1e-12)
    k = k * (1.0 + (a - 1.0) * ka_ref[...])
    aa = -kk
    bb = kk * a
    bonus = seg_sum(r * k * rk_ref[...]) * v

    tr = lax.broadcasted_iota(jnp.int32, (ch, ch), 0)
    ts = lax.broadcasted_iota(jnp.int32, (ch, ch), 1)
    tri_incl = jnp.where(tr >= ts, 1.0, 0.0).astype(BF16)
    gr = lax.broadcasted_iota(jnp.int32, (ch, gw), 0)
    gs = lax.broadcasted_iota(jnp.int32, (ch, gw), 1) % ch
    lower = gr >= gs
    strict = gr > gs
    eye = jnp.where(gr == gs, 1.0, 0.0)
    br = lax.broadcasted_iota(jnp.int32, (gw, gw), 0) // HEAD_DIM
    bc = lax.broadcasted_iota(jnp.int32, (gw, gw), 1) // HEAD_DIM
    same_head = br == bc

    def block_diag(t):
        tiled = jnp.concatenate([t] * (gw // ch), axis=0)
        return jnp.where(same_head, tiled, 0.0).astype(BF16)

    for c in range(tc // ch):
        sl = slice(c * ch, (c + 1) * ch)
        ld = log_decay[sl]
        cum = _dot_exact_lhs(tri_incl, ld)
        cum_end = cum[ch - 1:ch, :]
        p_incl = jnp.exp(cum)
        p_excl = jnp.exp(cum - ld)
        p_inv = jnp.exp(-cum)
        p_rest = jnp.exp(cum_end - cum)
        p_end = jnp.exp(cum_end)
        a_t = aa[sl] * p_excl
        r_t = r[sl] * p_incl
        b_t = bb[sl] * p_inv
        k_t = k[sl] * p_inv
        b_h = bb[sl] * p_rest
        k_h = k[sl] * p_rest
        for grp in range(n_groups):
            gl = slice(grp * gw, (grp + 1) * gw)
            g_state = state_ref[grp]
            ar = jnp.concatenate([a_t[:, gl], r_t[:, gl]], axis=0).astype(BF16)
            sb = _dot_nt(ar, block_diag(b_t[:, gl]))
            sk = _dot_nt(ar, block_diag(k_t[:, gl]))
            a_ab = jnp.where(strict, sb[:ch], 0.0)
            a_rb = jnp.where(lower, sb[ch:], 0.0)
            a_ak = jnp.where(strict, sk[:ch], 0.0)
            a_rk = jnp.where(lower, sk[ch:], 0.0)
            gh = _dot_nt(ar, g_state.astype(BF16))
            v_g = v[sl, gl]
            av = _dot(jnp.concatenate([a_ak, a_rk], axis=0).astype(BF16), block_diag(v_g))
            rhs_w = gh[:ch] + av[:ch]
            power = a_ab
            inv = eye
            levels = ch.bit_length() - 1
            for lvl in range(levels):
                rhs = block_diag(power)
                if lvl == 0:
                    inv = eye + a_ab
                    power = _dot(power.astype(BF16), rhs)
                elif lvl < levels - 1:
                    both = _dot(jnp.concatenate([power, inv], axis=0).astype(BF16), rhs)
                    power = both[:ch]
                    inv = inv + both[ch:]
                else:
                    inv = inv + _dot(inv.astype(BF16), rhs)
            u = _dot(inv.astype(BF16), block_diag(rhs_w))
            y = gh[ch:] + av[ch:] + _dot(a_rb.astype(BF16), block_diag(u))
            ybuf_ref[sl, gl] = y
            uv = jnp.concatenate([u, v_g], axis=0).astype(BF16)
            bk = jnp.concatenate([b_h[:, gl], k_h[:, gl]], axis=0).astype(BF16)
            upd = jnp.where(same_head, _dot_tn(uv, bk), 0.0)
            state_ref[grp] = g_state * p_end[:, gl] + upd

    y = ybuf_ref[...]
    inv_n = 1.0 / HEAD_DIM
    mean = seg_sum(y) * inv_n
    d = y - mean
    var = seg_sum(d * d) * inv_n
    yn = d * lax.rsqrt(var + GN_EPS) * lnw_ref[...] + lnb_ref[...]
    y_ref[...] = ((yn + bonus) * gate).astype(BF16)


def _rwkv_mix(rz, mu, w0, w2p, a0, a2p, g2, k_k, k_a, r_k, ln_w, ln_b, batch):
    n = rz.shape[0]
    t = n // batch
    tc = RWKV_TILE
    nt = t // tc
    vec = lambda w: _const_spec((1, w))
    return pl.pallas_call(
        _rwkv_kernel,
        grid=(batch, nt),
        in_specs=[pl.BlockSpec((tc, RWKV_IN), lambda b, i: (b * nt + i, 0)),
                  vec(RWKV_IN), vec(RWKV_WIDTH), _const_spec((LANES, RWKV_WIDTH)),
                  vec(RWKV_WIDTH), _const_spec((LANES, RWKV_WIDTH)),
                  _const_spec((GATE_LORA, RWKV_WIDTH)),
                  vec(RWKV_WIDTH), vec(RWKV_WIDTH), vec(RWKV_WIDTH), vec(RWKV_WIDTH),
                  vec(RWKV_WIDTH)],
        out_specs=pl.BlockSpec((tc, RWKV_WIDTH), lambda b, i: (b * nt + i, 0)),
        out_shape=jax.ShapeDtypeStruct((n, RWKV_WIDTH), BF16),
        scratch_shapes=[pltpu.VMEM((RWKV_WIDTH // RWKV_GROUP, RWKV_GROUP, RWKV_GROUP), F32),
                        pltpu.VMEM((8, RWKV_IN), F32),
                        pltpu.VMEM((tc, RWKV_WIDTH), F32)],
        compiler_params=_params("arbitrary", "arbitrary"),
        name="rwkv_mix",
    )(rz, mu, w0, w2p, a0, a2p, g2, k_k, k_a, r_k, ln_w, ln_b)


def _rope_table_kernel(pos_ref, freq_ref, cos_ref, sin_up_ref, sin_dn_ref):
    ang = pos_ref[...] * freq_ref[...]
    c = jnp.cos(ang)
    s = jnp.sin(ang)
    d = lax.broadcasted_iota(jnp.int32, ang.shape, 1) % HEAD_DIM
    half = ROPE_DIMS // 2
    cos_ref[...] = jnp.where(d < ROPE_DIMS, c, 1.0)
    sin_up_ref[...] = jnp.where(d < half, -s, 0.0)
    sin_dn_ref[...] = jnp.where((d >= half) & (d < ROPE_DIMS), s, 0.0)


def _rope_tables(positions):
    n = positions.size
    tm = 2048
    pos = jnp.broadcast_to(positions.reshape(n, 1).astype(F32), (n, LANES))
    half = ROPE_DIMS // 2
    inv_freq = ROPE_THETA ** (-jnp.arange(0, ROPE_DIMS, 2, dtype=F32) / ROPE_DIMS)
    d = jnp.arange(LANES) % HEAD_DIM
    freq = jnp.where(d < ROPE_DIMS, inv_freq[d % half], 0.0).reshape(1, LANES)
    row = pl.BlockSpec((tm, LANES), lambda i: (i, 0))
    return pl.pallas_call(
        _rope_table_kernel,
        grid=(n // tm,),
        in_specs=[row, _const_spec((1, LANES))],
        out_specs=[row, row, row],
        out_shape=[jax.ShapeDtypeStruct((n, LANES), F32)] * 3,
        compiler_params=_params("arbitrary"),
        name="rope_tables",
    )(pos, freq)


def _odd_in_kernel(x_ref, g_ref, w_ref, cos_ref, up_ref, dn_ref, q_ref, k_ref, v_ref):
    h = _rms_norm(x_ref[...], g_ref[...]).astype(BF16)
    reps = DIL_WIDTH // LANES
    cos = jnp.concatenate([cos_ref[...]] * reps, axis=1)
    up = jnp.concatenate([up_ref[...]] * reps, axis=1)
    dn = jnp.concatenate([dn_ref[...]] * reps, axis=1)
    half = ROPE_DIMS // 2

    def rotary(t):
        return (t * cos + pltpu.roll(t, DIL_WIDTH - half, axis=1) * up
                + pltpu.roll(t, half, axis=1) * dn)

    q = rotary(_dot(h, w_ref[:, 0:DIL_WIDTH]))
    q_ref[...] = (q * HEAD_DIM ** -0.5).astype(BF16)
    k_ref[...] = rotary(_dot(h, w_ref[:, DIL_WIDTH:2 * DIL_WIDTH])).astype(BF16)
    v_ref[...] = _dot(h, w_ref[:, 2 * DIL_WIDTH:]).astype(BF16)


def _odd_in(x2, g, w, cos, up, dn):
    n = x2.shape[0]
    tm = ROW_TILE
    row = lambda w_: pl.BlockSpec((tm, w_), lambda i: (i, 0))
    return pl.pallas_call(
        _odd_in_kernel,
        grid=(n // tm,),
        in_specs=[row(D_MODEL), _const_spec((1, D_MODEL)), _const_spec((D_MODEL, 3 * DIL_WIDTH)),
                  row(LANES), row(LANES), row(LANES)],
        out_specs=[row(DIL_WIDTH)] * 3,
        out_shape=[jax.ShapeDtypeStruct((n, DIL_WIDTH), BF16)] * 3,
        compiler_params=_params("arbitrary"),
        name="odd_in",
    )(x2, g, w, cos, up, dn)


def _dilated_kernel(q_ref, kc_ref, kp_ref, vc_ref, vp_ref, o_ref, lse_ref):
    blk = pl.program_id(2)
    sp = DIL_SPAN
    qrows = q_ref.shape[1]
    lane = lax.broadcasted_iota(jnp.int32, (sp, LANES), 1)
    qi = lax.broadcasted_iota(jnp.int32, (sp, 2 * sp), 0)
    ki = lax.broadcasted_iota(jnp.int32, (sp, 2 * sp), 1)
    band = (ki >= qi) & (ki <= qi + sp)
    band_first = band & (ki >= jnp.where(blk > 0, 0, sp))
    n_pairs = DIL_WIDTH // LANES

    for sub in range(qrows // sp):
        r0 = sub * sp

        def pair_body(p, lse_acc, r0=r0, sub=sub):
            c0 = pl.multiple_of(p * LANES, LANES)
            q = q_ref[0, r0:r0 + sp, pl.ds(c0, LANES)]
            if sub == 0:
                kb = jnp.concatenate([kp_ref[0, qrows - sp:qrows, pl.ds(c0, LANES)],
                                      kc_ref[0, 0:sp, pl.ds(c0, LANES)]], axis=0)
                vb = jnp.concatenate([vp_ref[0, qrows - sp:qrows, pl.ds(c0, LANES)],
                                      vc_ref[0, 0:sp, pl.ds(c0, LANES)]], axis=0)
                valid = band_first
            else:
                kb = kc_ref[0, r0 - sp:r0 + sp, pl.ds(c0, LANES)]
                vb = vc_ref[0, r0 - sp:r0 + sp, pl.ds(c0, LANES)]
                valid = band
            outs = []
            for hh in range(2):
                in_head = (lane >= hh * HEAD_DIM) & (lane < (hh + 1) * HEAD_DIM)
                qm = jnp.where(in_head, q, jnp.zeros_like(q))
                s = jnp.where(valid, _dot_nt(qm, kb), NEG_INF)
                m = jnp.max(s, axis=1, keepdims=True)
                e = jnp.exp(s - m)
                den = jnp.sum(e, axis=1, keepdims=True)
                outs.append(_dot((e / den).astype(BF16), vb))
                lse_acc = jnp.where(lane == 2 * p + hh, m + jnp.log(den), lse_acc)
            o_ref[0, r0:r0 + sp, pl.ds(c0, LANES)] = jnp.where(
                lane < HEAD_DIM, outs[0], outs[1]).astype(o_ref.dtype)
            return lse_acc

        lse = lax.fori_loop(0, n_pairs, pair_body, jnp.zeros((sp, LANES), F32))
        lse_ref[0, r0:r0 + sp, :] = lse


def _dilated_branch(q, k, v, dilation, batch):
    n = q.shape[0]
    t = n // batch
    length = t // dilation
    qrows = min(DIL_QROWS, length)
    nb = length // qrows
    view = lambda a: a.reshape(batch, length, dilation * DIL_WIDTH)
    cur = pl.BlockSpec((1, qrows, DIL_WIDTH), lambda b, r, i: (b, i, r))
    prev = pl.BlockSpec((1, qrows, DIL_WIDTH), lambda b, r, i: (b, jnp.maximum(i - 1, 0), r))
    o, lse = pl.pallas_call(
        _dilated_kernel,
        grid=(batch, dilation, nb),
        in_specs=[cur, cur, prev, cur, prev],
        out_specs=[cur, pl.BlockSpec((1, qrows, LANES), lambda b, r, i: (b, i, r))],
        out_shape=[jax.ShapeDtypeStruct((batch, length, dilation * DIL_WIDTH), BF16),
                   jax.ShapeDtypeStruct((batch, length, dilation * LANES), F32)],
        compiler_params=_params("arbitrary", "arbitrary", "arbitrary"),
        name=f"dilated_d{dilation}",
    )(view(q), view(k), view(k), view(v), view(v))
    return o.reshape(n, DIL_WIDTH), lse.reshape(n, LANES)


def _gelu_tanh(x):
    return 0.5 * x * (1.0 + jnp.tanh(0.7978845608028654 * (x + 0.044715 * x * x * x)))


def _finish_layer(mixed, x_ref, wo_ref, gmp_ref, gfp_ref, gfo_ref, wup_ref, cw_ref, cb_ref,
                  wdn_ref, out_ref, tail_ref, act_ref, tiles_per_batch):
    i = pl.program_id(0)
    tm = x_ref.shape[0]
    m = _dot(mixed, wo_ref[...])
    x1 = x_ref[...] + _rms_norm(m, gmp_ref[...])
    h = _rms_norm(x1, gfp_ref[...]).astype(BF16)

    @pl.when(i % tiles_per_batch == 0)
    def _():
        tail_ref[...] = jnp.zeros_like(tail_ref)

    ck = FFN_CHUNK
    rows = lax.broadcasted_iota(jnp.int32, (tm, ck), 0)

    def conv(col):
        u = _dot(h, wup_ref[:, col:col + ck])
        t1 = tail_ref[7:8, col:col + ck]
        t2 = tail_ref[6:7, col:col + ck]
        u1 = jnp.where(rows == 0, t1, pltpu.roll(u, 1, axis=0))
        u2 = jnp.where(rows == 0, t2, jnp.where(rows == 1, t1, pltpu.roll(u, 2, axis=0)))
        tail_ref[:, col:col + ck] = u[tm - 8:tm, :]
        return (cb_ref[:, col:col + ck] + cw_ref[2:3, col:col + ck] * u
                + cw_ref[1:2, col:col + ck] * u1 + cw_ref[0:1, col:col + ck] * u2)

    for c in range(D_FF // ck):
        gate = conv(c * ck)
        val = conv(D_FF + c * ck)
        act_ref[:, c * ck:(c + 1) * ck] = (_gelu_tanh(gate) * val).astype(BF16)
    f = _dot(act_ref[...], wdn_ref[...])
    out_ref[...] = x1 + _rms_norm(f, gfo_ref[...])


def _even_post_kernel(yf_ref, yr_ref, *rest, tiles_per_batch):
    mixed = jnp.concatenate([yf_ref[...], yr_ref[...]], axis=1)
    _finish_layer(mixed, *rest, tiles_per_batch=tiles_per_batch)


def _odd_post_kernel(o1_ref, o2_ref, o3_ref, l1_ref, l2_ref, l3_ref, *rest, tiles_per_batch):
    l1, l2, l3 = l1_ref[...], l2_ref[...], l3_ref[...]
    m = jnp.maximum(jnp.maximum(l1, l2), l3)
    e1, e2, e3 = jnp.exp(l1 - m), jnp.exp(l2 - m), jnp.exp(l3 - m)
    inv = 1.0 / (e1 + e2 + e3)
    hr = lax.broadcasted_iota(jnp.int32, (LANES, DIL_WIDTH), 0)
    hc = lax.broadcasted_iota(jnp.int32, (LANES, DIL_WIDTH), 1) // HEAD_DIM
    expand = jnp.where(hr == hc, 1.0, 0.0).astype(BF16)
    widen = lambda w: _dot_exact_rhs(w, expand)
    o = (widen(e1 * inv) * o1_ref[...].astype(F32) + widen(e2 * inv) * o2_ref[...].astype(F32)
         + widen(e3 * inv) * o3_ref[...].astype(F32))
    _finish_layer(o.astype(BF16), *rest, tiles_per_batch=tiles_per_batch)


def _post_ffn(mixer_outs, x2, wo, g_mix_post, g_ffn_pre, g_ffn_post, w_up, conv_w, conv_b, w_dn,
              batch, even):
    n = x2.shape[0]
    tm = ROW_TILE
    row = lambda w: pl.BlockSpec((tm, w), lambda i: (i, 0))
    vec = _const_spec((1, D_MODEL))
    mixer_specs = [row(a.shape[1]) for a in mixer_outs]
    body = _even_post_kernel if even else _odd_post_kernel
    return pl.pallas_call(
        functools.partial(body, tiles_per_batch=n // batch // tm),
        grid=(n // tm,),
        in_specs=mixer_specs + [row(D_MODEL), _const_spec((D_MODEL, D_MODEL)), vec, vec, vec,
                                _const_spec((D_MODEL, 2 * D_FF)), _const_spec((8, 2 * D_FF)),
                                _const_spec((1, 2 * D_FF)), _const_spec((D_FF, D_MODEL))],
        out_specs=row(D_MODEL),
        out_shape=jax.ShapeDtypeStruct((n, D_MODEL), F32),
        scratch_shapes=[pltpu.VMEM((8, 2 * D_FF), F32), pltpu.VMEM((tm, D_FF), BF16)],
        compiler_params=_params("arbitrary"),
        name="post_ffn_even" if even else "post_ffn_odd",
    )(*mixer_outs, x2, wo, g_mix_post, g_ffn_pre, g_ffn_post, w_up, conv_w, conv_b, w_dn)


def _pad_rows(a, rows, before=0):
    return jnp.pad(a, ((before, rows - a.shape[0] - before), (0, 0)))


def kernel(x, positions, norm_mix_pre, norm_mix_post, norm_ffn_pre, norm_ffn_post, even_w_in, fox_forget_bias, rwkv_mu, rwkv_w0, rwkv_w2, rwkv_a0, rwkv_a2, rwkv_g2, rwkv_k_k, rwkv_k_a, rwkv_r_k, rwkv_ln_w, rwkv_ln_b, even_w_out, odd_w_in, odd_w_out, ffn_w_up, ffn_conv_w, ffn_conv_b, ffn_w_down):
    batch, seq, _ = x.shape
    n = batch * seq
    depth = norm_mix_pre.shape[0]
    x2 = x.reshape(n, D_MODEL)
    rope = None
    vec = lambda a: a.reshape(1, -1)
    for layer in range(depth):
        i = layer // 2
        g_pre = vec(norm_mix_pre[layer])
        if layer % 2 == 0:
            w_in = even_w_in[i]
            f0 = 3 * FOX_WIDTH
            w_pack = jnp.concatenate(
                [w_in[:, :f0], jnp.pad(w_in[:, f0:FOX_IN], ((0, 0), (0, LANES - N_FOX_HEADS))),
                 w_in[:, FOX_IN:]], axis=1).astype(BF16)
            fb = jnp.pad(fox_forget_bias[i], (0, LANES - N_FOX_HEADS)).reshape(1, LANES)
            q, k, v, c_col, rz = _even_in(x2, g_pre, w_pack, fb, batch)
            c_row = c_col.reshape(batch, seq, LANES)[:, :, :N_FOX_HEADS].transpose(0, 2, 1)
            y_fox = _fox_attention(q, k, v, c_col, c_row, batch)
            w2p = _pad_rows(rwkv_w2[i], LANES).astype(BF16)
            a2p = _pad_rows(rwkv_a2[i], LANES, before=DECAY_LORA).astype(BF16)
            y_rwkv = _rwkv_mix(rz, vec(rwkv_mu[i]), vec(rwkv_w0[i]), w2p, vec(rwkv_a0[i]), a2p,
                               rwkv_g2[i].astype(BF16), vec(rwkv_k_k[i]), vec(rwkv_k_a[i]),
                               vec(rwkv_r_k[i]), vec(rwkv_ln_w[i]), vec(rwkv_ln_b[i]), batch)
            mixer_outs = [y_fox, y_rwkv]
            w_out = even_w_out[i]
        else:
            if rope is None:
                rope = _rope_tables(positions)
            q, k, v = _odd_in(x2, g_pre, odd_w_in[i].astype(BF16), *rope)
            outs, lses = zip(*[_dilated_branch(q, k, v, d, batch) for (_, d) in DILATED_GROUPS])
            mixer_outs = list(outs) + list(lses)
            w_out = odd_w_out[i]
        x2 = _post_ffn(mixer_outs, x2, w_out.astype(BF16), vec(norm_mix_post[layer]),
                       vec(norm_ffn_pre[layer]), vec(norm_ffn_post[layer]),
                       ffn_w_up[layer].astype(BF16), _pad_rows(ffn_conv_w[layer], 8),
                       vec(ffn_conv_b[layer]), ffn_w_down[layer].astype(BF16), batch,
                       even=layer % 2 == 0)
    return x2.reshape(batch, seq, D_MODEL)
```

```python
import functools

import jax
import jax.numpy as jnp
from jax import lax
from jax.experimental import pallas as pl
from jax.experimental.pallas import tpu as pltpu

F32 = jnp.float32
BF16 = jnp.bfloat16

D_MODEL = 1024
HEAD_DIM = 64
N_FOX_HEADS = 8
FOX_WIDTH = 512
RWKV_WIDTH = 512
DECAY_LORA = 64
ICLR_LORA = 64
GATE_LORA = 128
RWKV_IN = 3 * RWKV_WIDTH + DECAY_LORA + ICLR_LORA + GATE_LORA
FOX_IN = 3 * FOX_WIDTH + N_FOX_HEADS
N_DIL_HEADS = 16
DIL_WIDTH = 1024
DILATED_GROUPS = ((128, 1), (512, 4), (2048, 16))
DIL_SPAN = 128
ROPE_THETA = 500000.0
ROPE_DIMS = 16
D_FF = 2816
RMS_EPS = 1e-6
GN_EPS = 64e-5
NEG_INF = -1e30

LANES = 128
V7X_VMEM_LIMIT = 56 * 1024 * 1024

ROW_TILE = 512
FOX_TILE = 512
RWKV_TILE = 256
RWKV_CHUNK = 64
RWKV_GROUP = 256
DIL_QROWS = 512
DIL_PAIRS_PER_STEP = 4
FFN_CHUNK = 256


def _dot(a, b):
    return jnp.dot(a, b, preferred_element_type=F32)


def _dot_nt(a, b):
    return lax.dot_general(a, b, (((1,), (1,)), ((), ())), preferred_element_type=F32)


def _dot_tn(a, b):
    return lax.dot_general(a, b, (((0,), (0,)), ((), ())), preferred_element_type=F32)


def _split3(x):
    hi = x.astype(BF16)
    r1 = x - hi.astype(F32)
    mid = r1.astype(BF16)
    lo = (r1 - mid.astype(F32)).astype(BF16)
    return hi, mid, lo


def _dot_exact_lhs(sel, x):
    hi, mid, lo = _split3(x)
    return _dot(sel, hi) + _dot(sel, mid) + _dot(sel, lo)


def _dot_exact_rhs(x, sel):
    hi, mid, lo = _split3(x)
    return _dot(hi, sel) + _dot(mid, sel) + _dot(lo, sel)


def _rms_norm(x, g):
    return x * lax.rsqrt(jnp.mean(x * x, axis=-1, keepdims=True) + RMS_EPS) * g


def _softplus(x):
    return jnp.maximum(x, 0.0) + jnp.log1p(jnp.exp(-jnp.abs(x)))


def _sigmoid(x):
    return 1.0 / (1.0 + jnp.exp(-x))


def _const_spec(shape):
    nd = len(shape)
    return pl.BlockSpec(shape, lambda *_: (0,) * nd, pipeline_mode=pl.Buffered(1))


def _params(*sem):
    return pltpu.CompilerParams(dimension_semantics=sem, vmem_limit_bytes=V7X_VMEM_LIMIT)


EVEN_PACKED = 3 * FOX_WIDTH + LANES + RWKV_IN


def _even_in_kernel(x_ref, g_ref, w_ref, fb_ref, q_ref, k_ref, v_ref, c_ref, rz_ref, carry_ref,
                    *, tiles_per_batch):
    i = pl.program_id(0)
    tm = x_ref.shape[0]
    h = _rms_norm(x_ref[...], g_ref[...]).astype(BF16)
    hs = HEAD_DIM ** -0.5
    q_ref[...] = (_dot(h, w_ref[:, 0:FOX_WIDTH]) * hs).astype(BF16)
    k_ref[...] = _dot(h, w_ref[:, FOX_WIDTH:2 * FOX_WIDTH]).astype(BF16)
    v_ref[...] = _dot(h, w_ref[:, 2 * FOX_WIDTH:3 * FOX_WIDTH]).astype(BF16)
    f0 = 3 * FOX_WIDTH
    f = _dot(h, w_ref[:, f0:f0 + LANES]) + fb_ref[...]
    log_f = -_softplus(-f)

    @pl.when(i % tiles_per_batch == 0)
    def _():
        carry_ref[...] = jnp.zeros_like(carry_ref)

    row = lax.broadcasted_iota(jnp.int32, (tm, tm), 0)
    col = lax.broadcasted_iota(jnp.int32, (tm, tm), 1)
    tri = jnp.where(row >= col, 1.0, 0.0).astype(BF16)
    c = _dot_exact_lhs(tri, log_f) + carry_ref[0:1, :]
    c_ref[...] = c
    carry_ref[...] = jnp.broadcast_to(c[tm - 1:tm, :], carry_ref.shape)
    rz_ref[...] = _dot(h, w_ref[:, f0 + LANES:])


def _even_in(x2, g, w_pack, fb, batch):
    n = x2.shape[0]
    tm = ROW_TILE
    row = lambda w: pl.BlockSpec((tm, w), lambda i: (i, 0))
    return pl.pallas_call(
        functools.partial(_even_in_kernel, tiles_per_batch=n // batch // tm),
        grid=(n // tm,),
        in_specs=[row(D_MODEL), _const_spec((1, D_MODEL)), _const_spec((D_MODEL, EVEN_PACKED)),
                  _const_spec((1, LANES))],
        out_specs=[row(FOX_WIDTH), row(FOX_WIDTH), row(FOX_WIDTH), row(LANES), row(RWKV_IN)],
        out_shape=[jax.ShapeDtypeStruct((n, FOX_WIDTH), BF16)] * 3
        + [jax.ShapeDtypeStruct((n, LANES), F32), jax.ShapeDtypeStruct((n, RWKV_IN), F32)],
        scratch_shapes=[pltpu.VMEM((8, LANES), F32)],
        compiler_params=_params("arbitrary"),
        name="even_in",
    )(x2, g, w_pack, fb)


def _fox_kernel(q_ref, k_ref, v_ref, ccol_ref, crow_ref, o_ref):
    hp = pl.program_id(1)
    qi = pl.program_id(2)
    tq = q_ref.shape[0]
    tk = tq
    q = q_ref[...]
    lane = lax.broadcasted_iota(jnp.int32, (tq, LANES), 1)
    ccol = ccol_ref[...]
    heads = []
    for hh in range(2):
        in_head = (lane >= hh * HEAD_DIM) & (lane < (hh + 1) * HEAD_DIM)
        qm = jnp.where(in_head, q, jnp.zeros_like(q))
        h = hp * 2 + hh
        cq = jnp.sum(jnp.where(lane == h, ccol, 0.0), axis=1, keepdims=True)
        heads.append((qm, cq, h))
    rr = lax.broadcasted_iota(jnp.int32, (tq, tk), 0)
    cc = lax.broadcasted_iota(jnp.int32, (tq, tk), 1)
    causal = rr >= cc

    def step(j, carry, masked):
        k0 = pl.multiple_of(j * tk, tk)
        ks = k_ref[pl.ds(k0, tk), :]
        vs = v_ref[pl.ds(k0, tk), :]
        out = []
        for hh in range(2):
            qm, cq, h = heads[hh]
            m, l, acc = carry[hh]
            ck = crow_ref[0, pl.ds(h, 1), pl.ds(k0, tk)]
            s = _dot_nt(qm, ks) + cq - ck
            if masked:
                s = jnp.where(causal, s, NEG_INF)
            m_new = jnp.maximum(m, jnp.max(s, axis=1, keepdims=True))
            p = jnp.exp(s - m_new)
            alpha = jnp.exp(m - m_new)
            l = alpha * l + jnp.sum(p, axis=1, keepdims=True)
            acc = alpha * acc + _dot(p.astype(BF16), vs)
            out.append((m_new, l, acc))
        return tuple(out)

    init = tuple((jnp.full((tq, 1), NEG_INF, F32), jnp.zeros((tq, 1), F32),
                  jnp.zeros((tq, LANES), F32)) for _ in range(2))
    carry = lax.fori_loop(0, qi, lambda j, c: step(j, c, False), init)
    (_, l0, a0), (_, l1, a1) = step(qi, carry, True)
    o_ref[...] = jnp.where(lane < HEAD_DIM, a0 / l0, a1 / l1).astype(BF16)


def _fox_attention(q, k, v, c_col, c_row, batch):
    n = q.shape[0]
    t = n // batch
    tq = FOX_TILE
    nq = t // tq
    pairs = FOX_WIDTH // LANES
    return pl.pallas_call(
        _fox_kernel,
        grid=(batch, pairs, nq),
        in_specs=[
            pl.BlockSpec((tq, LANES), lambda b, p, i: (b * nq + i, p)),
            pl.BlockSpec((t, LANES), lambda b, p, i: (b, p)),
            pl.BlockSpec((t, LANES), lambda b, p, i: (b, p)),
            pl.BlockSpec((tq, LANES), lambda b, p, i: (b * nq + i, 0)),
            pl.BlockSpec((1, N_FOX_HEADS, t), lambda b, p, i: (b, 0, 0)),
        ],
        out_specs=pl.BlockSpec((tq, LANES), lambda b, p, i: (b * nq + i, p)),
        out_shape=jax.ShapeDtypeStruct((n, FOX_WIDTH), BF16),
        compiler_params=_params("arbitrary", "arbitrary", "arbitrary"),
        name="fox_attention",
    )(q, k, v, c_col, c_row)


def _rwkv_kernel(rz_ref, mu_ref, w0_ref, w2_ref, a0_ref, a2_ref, g2_ref, kk_ref, ka_ref, rk_ref,
                 lnw_ref, lnb_ref, y_ref, state_ref, tail_ref, ybuf_ref):
    tc = rz_ref.shape[0]
    ch = RWKV_CHUNK
    gw = RWKV_GROUP
    n_groups = RWKV_WIDTH // gw

    @pl.when(pl.program_id(1) == 0)
    def _():
        state_ref[...] = jnp.zeros_like(state_ref)
        tail_ref[...] = jnp.zeros_like(tail_ref)

    z = rz_ref[...]
    rows = lax.broadcasted_iota(jnp.int32, z.shape, 0)
    z_prev = jnp.where(rows == 0, tail_ref[7:8, :], pltpu.roll(z, 1, axis=0))
    tail_ref[...] = z[tc - 8:tc, :]
    z = z + (z_prev - z) * mu_ref[...]
    r = z[:, 0:RWKV_WIDTH]
    k = z[:, RWKV_WIDTH:2 * RWKV_WIDTH]
    v = z[:, 2 * RWKV_WIDTH:3 * RWKV_WIDTH]
    lo = z[:, 3 * RWKV_WIDTH:3 * RWKV_WIDTH + LANES]
    g_lo = z[:, 3 * RWKV_WIDTH + LANES:]

    w = w0_ref[...] + _dot(jnp.tanh(lo).astype(BF16), w2_ref[...])
    w = -_softplus(-w) - 0.5
    log_decay = -jnp.exp(w)
    a = _sigmoid(a0_ref[...] + _dot(lo.astype(BF16), a2_ref[...]))
    gate = _dot(_sigmoid(g_lo).astype(BF16), g2_ref[...])

    hr = lax.broadcasted_iota(jnp.int32, (RWKV_WIDTH, RWKV_WIDTH), 0) // HEAD_DIM
    hc = lax.broadcasted_iota(jnp.int32, (RWKV_WIDTH, RWKV_WIDTH), 1) // HEAD_DIM
    head_ones = jnp.where(hr == hc, 1.0, 0.0).astype(BF16)
    seg_sum = lambda t: _dot_exact_rhs(t, head_ones)

    kk = k * kk_ref[...]
    kk = kk / jnp.maximum(jnp.sqrt(seg_sum(kk * kk)), 1e-12)
    k = k * (1.0 + (a - 1.0) * ka_ref[...])
    aa = -kk
    bb = kk * a
    bonus = seg_sum(r * k * rk_ref[...]) * v

    tr = lax.broadcasted_iota(jnp.int32, (ch, ch), 0)
    ts = lax.broadcasted_iota(jnp.int32, (ch, ch), 1)
    tri_incl = jnp.where(tr >= ts, 1.0, 0.0).astype(BF16)
    gr = lax.broadcasted_iota(jnp.int32, (ch, gw), 0)
    gs = lax.broadcasted_iota(jnp.int32, (ch, gw), 1) % ch
    lower = gr >= gs
    strict = gr > gs
    eye = jnp.where(gr == gs, 1.0, 0.0)
    br = lax.broadcasted_iota(jnp.int32, (gw, gw), 0) // HEAD_DIM
    bc = lax.broadcasted_iota(jnp.int32, (gw, gw), 1) // HEAD_DIM
    same_head = br == bc

    def block_diag(t):
        tiled = jnp.concatenate([t] * (gw // ch), axis=0)
        return jnp.where(same_head, tiled, 0.0).astype(BF16)

    for c in range(tc // ch):
        sl = slice(c * ch, (c + 1) * ch)
        ld = log_decay[sl]
        cum = _dot_exact_lhs(tri_incl, ld)
        cum_end = cum[ch - 1:ch, :]
        p_incl = jnp.exp(cum)
        p_excl = jnp.exp(cum - ld)
        p_inv = jnp.exp(-cum)
        p_rest = jnp.exp(cum_end - cum)
        p_end = jnp.exp(cum_end)
        a_t = aa[sl] * p_excl
        r_t = r[sl] * p_incl
        b_t = bb[sl] * p_inv
        k_t = k[sl] * p_inv
        b_h = bb[sl] * p_rest
        k_h = k[sl] * p_rest
        for grp in range(n_groups):
            gl = slice(grp * gw, (grp + 1) * gw)
            g_state = state_ref[grp]
            ar = jnp.concatenate([a_t[:, gl], r_t[:, gl]], axis=0).astype(BF16)
            sb = _dot_nt(ar, block_diag(b_t[:, gl]))
            sk = _dot_nt(ar, block_diag(k_t[:, gl]))
            a_ab = jnp.where(strict, sb[:ch], 0.0)
            a_rb = jnp.where(lower, sb[ch:], 0.0)
            a_ak = jnp.where(strict, sk[:ch], 0.0)
            a_rk = jnp.where(lower, sk[ch:], 0.0)
            gh = _dot_nt(ar, g_state.astype(BF16))
            v_g = v[sl, gl]
            av = _dot(jnp.concatenate([a_ak, a_rk], axis=0).astype(BF16), block_diag(v_g))
            rhs_w = gh[:ch] + av[:ch]
            power = a_ab
            inv = eye
            levels = ch.bit_length() - 1
            for lvl in range(levels):
                rhs = block_diag(power)
                if lvl == 0:
                    inv = eye + a_ab
                    power = _dot(power.astype(BF16), rhs)
                elif lvl < levels - 1:
                    both = _dot(jnp.concatenate([power, inv], axis=0).astype(BF16), rhs)
                    power = both[:ch]
                    inv = inv + both[ch:]
                else:
                    inv = inv + _dot(inv.astype(BF16), rhs)
            u = _dot(inv.astype(BF16), block_diag(rhs_w))
            y = gh[ch:] + av[ch:] + _dot(a_rb.astype(BF16), block_diag(u))
            ybuf_ref[sl, gl] = y
            uv = jnp.concatenate([u, v_g], axis=0).astype(BF16)
            bk = jnp.concatenate([b_h[:, gl], k_h[:, gl]], axis=0).astype(BF16)
            upd = jnp.where(same_head, _dot_tn(uv, bk), 0.0)
            state_ref[grp] = g_state * p_end[:, gl] + upd

    y = ybuf_ref[...]
    inv_n = 1.0 / HEAD_DIM
    mean = seg_sum(y) * inv_n
    d = y - mean
    var = seg_sum(d * d) * inv_n
    yn = d * lax.rsqrt(var + GN_EPS) * lnw_ref[...] + lnb_ref[...]
    y_ref[...] = ((yn + bonus) * gate).astype(BF16)


def _rwkv_mix(rz, mu, w0, w2p, a0, a2p, g2, k_k, k_a, r_k, ln_w, ln_b, batch):
    n = rz.shape[0]
    t = n // batch
    tc = RWKV_TILE
    nt = t // tc
    vec = lambda w: _const_spec((1, w))
    return pl.pallas_call(
        _rwkv_kernel,
        grid=(batch, nt),
        in_specs=[pl.BlockSpec((tc, RWKV_IN), lambda b, i: (b * nt + i, 0)),
                  vec(RWKV_IN), vec(RWKV_WIDTH), _const_spec((LANES, RWKV_WIDTH)),
                  vec(RWKV_WIDTH), _const_spec((LANES, RWKV_WIDTH)),
                  _const_spec((GATE_LORA, RWKV_WIDTH)),
                  vec(RWKV_WIDTH), vec(RWKV_WIDTH), vec(RWKV_WIDTH), vec(RWKV_WIDTH),
                  vec(RWKV_WIDTH)],
        out_specs=pl.BlockSpec((tc, RWKV_WIDTH), lambda b, i: (b * nt + i, 0)),
        out_shape=jax.ShapeDtypeStruct((n, RWKV_WIDTH), BF16),
        scratch_shapes=[pltpu.VMEM((RWKV_WIDTH // RWKV_GROUP, RWKV_GROUP, RWKV_GROUP), F32),
                        pltpu.VMEM((8, RWKV_IN), F32),
                        pltpu.VMEM((tc, RWKV_WIDTH), F32)],
        compiler_params=_params("arbitrary", "arbitrary"),
        name="rwkv_mix",
    )(rz, mu, w0, w2p, a0, a2p, g2, k_k, k_a, r_k, ln_w, ln_b)


def _rope_table_kernel(pos_ref, freq_ref, cos_ref, sin_up_ref, sin_dn_ref):
    ang = pos_ref[...] * freq_ref[...]
    c = jnp.cos(ang)
    s = jnp.sin(ang)
    d = lax.broadcasted_iota(jnp.int32, ang.shape, 1) % HEAD_DIM
    half = ROPE_DIMS // 2
    cos_ref[...] = jnp.where(d < ROPE_DIMS, c, 1.0)
    sin_up_ref[...] = jnp.where(d < half, -s, 0.0)
    sin_dn_ref[...] = jnp.where((d >= half) & (d < ROPE_DIMS), s, 0.0)


def _rope_tables(positions):
    n = positions.size
    tm = 2048
    pos = jnp.broadcast_to(positions.reshape(n, 1).astype(F32), (n, LANES))
    half = ROPE_DIMS // 2
    inv_freq = ROPE_THETA ** (-jnp.arange(0, ROPE_DIMS, 2, dtype=F32) / ROPE_DIMS)
    d = jnp.arange(LANES) % HEAD_DIM
    freq = jnp.where(d < ROPE_DIMS, inv_freq[d % half], 0.0).reshape(1, LANES)
    row = pl.BlockSpec((tm, LANES), lambda i: (i, 0))
    return pl.pallas_call(
        _rope_table_kernel,
        grid=(n // tm,),
        in_specs=[row, _const_spec((1, LANES))],
        out_specs=[row, row, row],
        out_shape=[jax.ShapeDtypeStruct((n, LANES), F32)] * 3,
        compiler_params=_params("arbitrary"),
        name="rope_tables",
    )(pos, freq)


def _odd_in_kernel(x_ref, g_ref, w_ref, cos_ref, up_ref, dn_ref, *refs):
    outs, perm_ref = refs[:-1], refs[-1]
    tm = x_ref.shape[0]
    h = _rms_norm(x_ref[...], g_ref[...]).astype(BF16)
    reps = DIL_WIDTH // LANES
    cos = jnp.concatenate([cos_ref[...]] * reps, axis=1)
    up = jnp.concatenate([up_ref[...]] * reps, axis=1)
    dn = jnp.concatenate([dn_ref[...]] * reps, axis=1)
    half = ROPE_DIMS // 2

    def rotary(t):
        return (t * cos + pltpu.roll(t, DIL_WIDTH - half, axis=1) * up
                + pltpu.roll(t, half, axis=1) * dn)

    def emit(idx, val):
        outs[idx][...] = val.astype(BF16)
        for c in range(reps):
            perm_ref[c] = val[:, c * LANES:(c + 1) * LANES]
        for gi, (_, d) in enumerate(DILATED_GROUPS[1:], start=1):
            ref = outs[gi * 3 + idx]
            for rho in range(d):
                for c in range(reps):
                    ref[0, rho, :, c * LANES:(c + 1) * LANES] = perm_ref[
                        c, pl.ds(rho, tm // d, stride=d), :].astype(BF16)

    emit(0, rotary(_dot(h, w_ref[:, 0:DIL_WIDTH])) * HEAD_DIM ** -0.5)
    emit(1, rotary(_dot(h, w_ref[:, DIL_WIDTH:2 * DIL_WIDTH])))
    emit(2, _dot(h, w_ref[:, 2 * DIL_WIDTH:]))


def _residue_spec(d, tm, width, tiles_per_batch):
    return pl.BlockSpec((1, d, tm // d, width),
                        lambda i: (i // tiles_per_batch, 0, i % tiles_per_batch, 0))


def _odd_in(x2, g, w, cos, up, dn, batch):
    n = x2.shape[0]
    t = n // batch
    tm = ROW_TILE
    row = lambda w_: pl.BlockSpec((tm, w_), lambda i: (i, 0))
    out_specs = [row(DIL_WIDTH)] * 3
    out_shape = [jax.ShapeDtypeStruct((n, DIL_WIDTH), BF16)] * 3
    for _, d in DILATED_GROUPS[1:]:
        out_specs += [_residue_spec(d, tm, DIL_WIDTH, t // tm)] * 3
        out_shape += [jax.ShapeDtypeStruct((batch, d, t // d, DIL_WIDTH), BF16)] * 3
    return pl.pallas_call(
        _odd_in_kernel,
        grid=(n // tm,),
        in_specs=[row(D_MODEL), _const_spec((1, D_MODEL)), _const_spec((D_MODEL, 3 * DIL_WIDTH)),
                  row(LANES), row(LANES), row(LANES)],
        out_specs=out_specs,
        out_shape=out_shape,
        scratch_shapes=[pltpu.VMEM((DIL_WIDTH // LANES, tm, LANES), F32)],
        compiler_params=_params("arbitrary"),
        name="odd_in",
    )(x2, g, w, cos, up, dn)


def _dilated_kernel(q_ref, kc_ref, kp_ref, vc_ref, vp_ref, o_ref, lse_ref):
    blk = pl.program_id(2)
    sp = DIL_SPAN
    qrows = q_ref.shape[2]
    lane = lax.broadcasted_iota(jnp.int32, (sp, LANES), 1)
    head0 = lane < HEAD_DIM
    qi = lax.broadcasted_iota(jnp.int32, (2 * sp, 2 * sp), 0) % sp
    ki = lax.broadcasted_iota(jnp.int32, (2 * sp, 2 * sp), 1)
    band = (ki >= qi) & (ki <= qi + sp)
    bias = jnp.where(band, 0.0, NEG_INF)
    bias_first = jnp.where(band & (ki >= jnp.where(blk > 0, 0, sp)), 0.0, NEG_INF)
    n_steps = DIL_WIDTH // LANES // DIL_PAIRS_PER_STEP

    for sub in range(qrows // sp):
        r0 = sub * sp

        def step(g, lse_acc, r0=r0, sub=sub):
            for pp in range(DIL_PAIRS_PER_STEP):
                p = g * DIL_PAIRS_PER_STEP + pp
                cols = pl.ds(pl.multiple_of(p * LANES, LANES), LANES)
                q = q_ref[0, 0, r0:r0 + sp, cols]
                if sub == 0:
                    kb = jnp.concatenate([kp_ref[0, 0, qrows - sp:qrows, cols],
                                          kc_ref[0, 0, 0:sp, cols]], axis=0)
                    vb = jnp.concatenate([vp_ref[0, 0, qrows - sp:qrows, cols],
                                          vc_ref[0, 0, 0:sp, cols]], axis=0)
                else:
                    kb = kc_ref[0, 0, r0 - sp:r0 + sp, cols]
                    vb = vc_ref[0, 0, r0 - sp:r0 + sp, cols]
                zero = jnp.zeros_like(q)
                q2 = jnp.concatenate([jnp.where(head0, q, zero), jnp.where(head0, zero, q)], axis=0)
                s = _dot_nt(q2, kb) + (bias_first if sub == 0 else bias)
                m = jnp.max(s, axis=1, keepdims=True)
                e = jnp.exp(s - m)
                den = jnp.sum(e, axis=1, keepdims=True)
                o2 = _dot(e.astype(BF16), vb) * (1.0 / den)
                lse = m + jnp.log(den)
                o_ref[0, 0, r0:r0 + sp, cols] = jnp.where(head0, o2[:sp], o2[sp:]).astype(o_ref.dtype)
                lse_acc = jnp.where(lane == 2 * p, lse[:sp],
                                    jnp.where(lane == 2 * p + 1, lse[sp:], lse_acc))
            return lse_acc

        lse_all = lax.fori_loop(0, n_steps, step, jnp.zeros((sp, LANES), F32))
        lse_ref[0, 0, r0:r0 + sp, :] = lse_all


def _dilated_branch(q, k, v, batch):
    _, dilation, length, _ = q.shape
    qrows = min(DIL_QROWS, length)
    nb = length // qrows
    cur = pl.BlockSpec((1, 1, qrows, DIL_WIDTH), lambda b, r, i: (b, r, i, 0))
    prev = pl.BlockSpec((1, 1, qrows, DIL_WIDTH), lambda b, r, i: (b, r, jnp.maximum(i - 1, 0), 0))
    return pl.pallas_call(
        _dilated_kernel,
        grid=(batch, dilation, nb),
        in_specs=[cur, cur, prev, cur, prev],
        out_specs=[cur, pl.BlockSpec((1, 1, qrows, LANES), lambda b, r, i: (b, r, i, 0))],
        out_shape=[jax.ShapeDtypeStruct((batch, dilation, length, DIL_WIDTH), BF16),
                   jax.ShapeDtypeStruct((batch, dilation, length, LANES), F32)],
        compiler_params=_params("arbitrary", "arbitrary", "arbitrary"),
        name=f"dilated_d{dilation}",
    )(q, k, k, v, v)


def _gelu_tanh(x):
    return 0.5 * x * (1.0 + jnp.tanh(0.7978845608028654 * (x + 0.044715 * x * x * x)))


def _finish_layer(mixed, x_ref, wo_ref, gmp_ref, gfp_ref, gfo_ref, wup_ref, cw_ref, cb_ref,
                  wdn_ref, out_ref, tail_ref, act_ref, tiles_per_batch):
    i = pl.program_id(0)
    tm = x_ref.shape[0]
    m = _dot(mixed, wo_ref[...])
    x1 = x_ref[...] + _rms_norm(m, gmp_ref[...])
    h = _rms_norm(x1, gfp_ref[...]).astype(BF16)

    @pl.when(i % tiles_per_batch == 0)
    def _():
        tail_ref[...] = jnp.zeros_like(tail_ref)

    ck = FFN_CHUNK
    rows = lax.broadcasted_iota(jnp.int32, (tm, ck), 0)

    def conv(col):
        u = _dot(h, wup_ref[:, col:col + ck])
        t1 = tail_ref[7:8, col:col + ck]
        t2 = tail_ref[6:7, col:col + ck]
        u1 = jnp.where(rows == 0, t1, pltpu.roll(u, 1, axis=0))
        u2 = jnp.where(rows == 0, t2, jnp.where(rows == 1, t1, pltpu.roll(u, 2, axis=0)))
        tail_ref[:, col:col + ck] = u[tm - 8:tm, :]
        return (cb_ref[:, col:col + ck] + cw_ref[2:3, col:col + ck] * u
                + cw_ref[1:2, col:col + ck] * u1 + cw_ref[0:1, col:col + ck] * u2)

    for c in range(D_FF // ck):
        gate = conv(c * ck)
        val = conv(D_FF + c * ck)
        act_ref[:, c * ck:(c + 1) * ck] = (_gelu_tanh(gate) * val).astype(BF16)
    f = _dot(act_ref[...], wdn_ref[...])
    out_ref[...] = x1 + _rms_norm(f, gfo_ref[...])


def _even_post_kernel(yf_ref, yr_ref, *rest, tiles_per_batch):
    mixed = jnp.concatenate([yf_ref[...], yr_ref[...]], axis=1)
    _finish_layer(mixed, *rest, tiles_per_batch=tiles_per_batch)


def _time_order(src_ref, perm_ref):
    _, d, rows, width = src_ref.shape
    if d == 1:
        return src_ref[0, 0].astype(F32)
    blocks = width // LANES
    for rho in range(d):
        for c in range(blocks):
            perm_ref[c, pl.ds(rho, rows, stride=d), :] = src_ref[
                0, rho, :, c * LANES:(c + 1) * LANES].astype(F32)
    return jnp.concatenate([perm_ref[c] for c in range(blocks)], axis=1)


def _odd_post_kernel(*refs, tiles_per_batch):
    ng = len(DILATED_GROUPS)
    o_refs, l_refs, rest = refs[:ng], refs[ng:2 * ng], refs[2 * ng:-2]
    perm_o_ref, perm_l_ref = refs[-2:]
    lses = [_time_order(l_ref, perm_l_ref) for l_ref in l_refs]
    m = functools.reduce(jnp.maximum, lses)
    es = [jnp.exp(l - m) for l in lses]
    inv = 1.0 / functools.reduce(jnp.add, es)
    hr = lax.broadcasted_iota(jnp.int32, (LANES, DIL_WIDTH), 0)
    hc = lax.broadcasted_iota(jnp.int32, (LANES, DIL_WIDTH), 1) // HEAD_DIM
    expand = jnp.where(hr == hc, 1.0, 0.0).astype(BF16)
    o = None
    for e, o_ref in zip(es, o_refs):
        term = _dot_exact_rhs(e * inv, expand) * _time_order(o_ref, perm_o_ref)
        o = term if o is None else o + term
    _finish_layer(o.astype(BF16), *rest, tiles_per_batch=tiles_per_batch)


def _post_ffn(mixer_outs, x2, wo, g_mix_post, g_ffn_pre, g_ffn_post, w_up, conv_w, conv_b, w_dn,
              batch, even):
    n = x2.shape[0]
    tm = ROW_TILE
    tpb = n // batch // tm
    row = lambda w: pl.BlockSpec((tm, w), lambda i: (i, 0))
    vec = _const_spec((1, D_MODEL))
    mixer_specs = [row(a.shape[1]) if a.ndim == 2 else _residue_spec(a.shape[1], tm, a.shape[3], tpb)
                   for a in mixer_outs]
    body = _even_post_kernel if even else _odd_post_kernel
    scratch = [pltpu.VMEM((8, 2 * D_FF), F32), pltpu.VMEM((tm, D_FF), BF16)]
    if not even:
        scratch += [pltpu.VMEM((DIL_WIDTH // LANES, tm, LANES), F32), pltpu.VMEM((1, tm, LANES), F32)]
    return pl.pallas_call(
        functools.partial(body, tiles_per_batch=tpb),
        grid=(n // tm,),
        in_specs=mixer_specs + [row(D_MODEL), _const_spec((D_MODEL, D_MODEL)), vec, vec, vec,
                                _const_spec((D_MODEL, 2 * D_FF)), _const_spec((8, 2 * D_FF)),
                                _const_spec((1, 2 * D_FF)), _const_spec((D_FF, D_MODEL))],
        out_specs=row(D_MODEL),
        out_shape=jax.ShapeDtypeStruct((n, D_MODEL), F32),
        scratch_shapes=scratch,
        compiler_params=_params("arbitrary"),
        name="post_ffn_even" if even else "post_ffn_odd",
    )(*mixer_outs, x2, wo, g_mix_post, g_ffn_pre, g_ffn_post, w_up, conv_w, conv_b, w_dn)


def _pad_rows(a, rows, before=0):
    return jnp.pad(a, ((before, rows - a.shape[0] - before), (0, 0)))


def kernel(x, positions, norm_mix_pre, norm_mix_post, norm_ffn_pre, norm_ffn_post, even_w_in, fox_forget_bias, rwkv_mu, rwkv_w0, rwkv_w2, rwkv_a0, rwkv_a2, rwkv_g2, rwkv_k_k, rwkv_k_a, rwkv_r_k, rwkv_ln_w, rwkv_ln_b, even_w_out, odd_w_in, odd_w_out, ffn_w_up, ffn_conv_w, ffn_conv_b, ffn_w_down):
    batch, seq, _ = x.shape
    n = batch * seq
    depth = norm_mix_pre.shape[0]
    x2 = x.reshape(n, D_MODEL)
    rope = None
    vec = lambda a: a.reshape(1, -1)
    for layer in range(depth):
        i = layer // 2
        g_pre = vec(norm_mix_pre[layer])
        if layer % 2 == 0:
            w_in = even_w_in[i]
            f0 = 3 * FOX_WIDTH
            w_pack = jnp.concatenate(
                [w_in[:, :f0], jnp.pad(w_in[:, f0:FOX_IN], ((0, 0), (0, LANES - N_FOX_HEADS))),
                 w_in[:, FOX_IN:]], axis=1).astype(BF16)
            fb = jnp.pad(fox_forget_bias[i], (0, LANES - N_FOX_HEADS)).reshape(1, LANES)
            q, k, v, c_col, rz = _even_in(x2, g_pre, w_pack, fb, batch)
            c_row = c_col.reshape(batch, seq, LANES)[:, :, :N_FOX_HEADS].transpose(0, 2, 1)
            y_fox = _fox_attention(q, k, v, c_col, c_row, batch)
            w2p = _pad_rows(rwkv_w2[i], LANES).astype(BF16)
            a2p = _pad_rows(rwkv_a2[i], LANES, before=DECAY_LORA).astype(BF16)
            y_rwkv = _rwkv_mix(rz, vec(rwkv_mu[i]), vec(rwkv_w0[i]), w2p, vec(rwkv_a0[i]), a2p,
                               rwkv_g2[i].astype(BF16), vec(rwkv_k_k[i]), vec(rwkv_k_a[i]),
                               vec(rwkv_r_k[i]), vec(rwkv_ln_w[i]), vec(rwkv_ln_b[i]), batch)
            mixer_outs = [y_fox, y_rwkv]
            w_out = even_w_out[i]
        else:
            if rope is None:
                rope = _rope_tables(positions)
            qkv = _odd_in(x2, g_pre, odd_w_in[i].astype(BF16), *rope, batch)
            outs, lses = [], []
            for gi, (_, d) in enumerate(DILATED_GROUPS):
                q, k, v = [a.reshape(batch, d, -1, DIL_WIDTH) for a in qkv[3 * gi:3 * gi + 3]]
                o, lse = _dilated_branch(q, k, v, batch)
                outs.append(o)
                lses.append(lse)
            mixer_outs = outs + lses
            w_out = odd_w_out[i]
        x2 = _post_ffn(mixer_outs, x2, w_out.astype(BF16), vec(norm_mix_post[layer]),
                       vec(norm_ffn_pre[layer]), vec(norm_ffn_post[layer]),
                       ffn_w_up[layer].astype(BF16), _pad_rows(ffn_conv_w[layer], 8),
                       vec(ffn_conv_b[layer]), ffn_w_down[layer].astype(BF16), batch,
                       even=layer % 2 == 0)
    return x2.reshape(batch, seq, D_MODEL)
```

```python
import functools

import jax
import jax.numpy as jnp
from jax import lax
from jax.experimental import pallas as pl
from jax.experimental.pallas import tpu as pltpu

F32 = jnp.float32
BF16 = jnp.bfloat16

D_MODEL = 1024
HEAD_DIM = 64
N_FOX_HEADS = 8
FOX_WIDTH = 512
RWKV_WIDTH = 512
DECAY_LORA = 64
ICLR_LORA = 64
GATE_LORA = 128
RWKV_IN = 3 * RWKV_WIDTH + DECAY_LORA + ICLR_LORA + GATE_LORA
FOX_IN = 3 * FOX_WIDTH + N_FOX_HEADS
N_DIL_HEADS = 16
DIL_WIDTH = 1024
DILATED_GROUPS = ((128, 1), (512, 4), (2048, 16))
DIL_SPAN = 128
ROPE_THETA = 500000.0
ROPE_DIMS = 16
D_FF = 2816
RMS_EPS = 1e-6
GN_EPS = 64e-5
NEG_INF = -1e30

LANES = 128
V7X_VMEM_LIMIT = 56 * 1024 * 1024

ROW_TILE = 512
FOX_TILE = 512
RWKV_TILE = 256
RWKV_CHUNK = 64
RWKV_GROUP = 256
DIL_QROWS = 512
DIL_PAIRS_PER_STEP = 4
FFN_CHUNK = 256


def _dot(a, b):
    return jnp.dot(a, b, preferred_element_type=F32)


def _dot_nt(a, b):
    return lax.dot_general(a, b, (((1,), (1,)), ((), ())), preferred_element_type=F32)


def _dot_tn(a, b):
    return lax.dot_general(a, b, (((0,), (0,)), ((), ())), preferred_element_type=F32)


def _split(x, terms):
    parts = []
    for _ in range(terms):
        part = x.astype(BF16)
        parts.append(part)
        x = x - part.astype(F32)
    return parts


def _dot_exact_lhs(sel, x, terms=3):
    return functools.reduce(jnp.add, [_dot(sel, part) for part in _split(x, terms)])


def _dot_exact_rhs(x, sel, terms=3):
    return functools.reduce(jnp.add, [_dot(part, sel) for part in _split(x, terms)])


def _rms_norm(x, g):
    return x * lax.rsqrt(jnp.mean(x * x, axis=-1, keepdims=True) + RMS_EPS) * g


def _softplus(x):
    return jnp.maximum(x, 0.0) + jnp.log1p(jnp.exp(-jnp.abs(x)))


def _sigmoid(x):
    return 1.0 / (1.0 + jnp.exp(-x))


def _const_spec(shape):
    nd = len(shape)
    return pl.BlockSpec(shape, lambda *_: (0,) * nd, pipeline_mode=pl.Buffered(1))


def _params(*sem):
    return pltpu.CompilerParams(dimension_semantics=sem, vmem_limit_bytes=V7X_VMEM_LIMIT)


EVEN_PACKED = FOX_WIDTH + LANES + RWKV_IN

LOG2_E = 1.4426950408889634
FOX_PAIRS = FOX_WIDTH // LANES
FOX_KEY_LANES = 2 * LANES
FOX_BIAS_TERMS = 3


def _fox_key_bias_selectors():
    r = lax.broadcasted_iota(jnp.int32, (LANES, FOX_PAIRS * LANES), 0)
    cidx = lax.broadcasted_iota(jnp.int32, (LANES, FOX_PAIRS * LANES), 1)
    pair, slot = cidx // LANES, cidx % LANES
    return [jnp.where((r // 2 == pair) & (r < N_FOX_HEADS)
                      & (slot == FOX_BIAS_TERMS * (r % 2) + t), 1.0, 0.0).astype(BF16)
            for t in range(FOX_BIAS_TERMS)]


def _even_in_kernel(x_ref, g_ref, w_ref, wt_ref, fb_ref, qt_ref, kx_ref, vt_ref, ct_ref, rz_ref,
                    carry_ref, *, tiles_per_batch):
    i = pl.program_id(0)
    tm = x_ref.shape[0]
    h = _rms_norm(x_ref[...], g_ref[...]).astype(BF16)
    qt_ref[...] = (_dot_nt(wt_ref[0:FOX_WIDTH, :], h) * (LOG2_E * HEAD_DIM ** -0.5)).astype(BF16)
    vt_ref[...] = _dot_nt(wt_ref[FOX_WIDTH:, :], h).astype(BF16)
    f = _dot(h, w_ref[:, FOX_WIDTH:FOX_WIDTH + LANES]) + fb_ref[...]
    log2_f = -_softplus(-f) * LOG2_E

    @pl.when(i % tiles_per_batch == 0)
    def _():
        carry_ref[...] = jnp.zeros_like(carry_ref)

    row = lax.broadcasted_iota(jnp.int32, (tm, tm), 0)
    col = lax.broadcasted_iota(jnp.int32, (tm, tm), 1)
    tri = jnp.where(row >= col, 1.0, 0.0).astype(BF16)
    c = _dot_exact_lhs(tri, log2_f) + carry_ref[0:1, :]
    carry_ref[...] = jnp.broadcast_to(c[tm - 1:tm, :], carry_ref.shape)
    ct_ref[...] = jnp.transpose(c)[0:N_FOX_HEADS, :]

    k = _dot(h, w_ref[:, 0:FOX_WIDTH])
    bias = functools.reduce(jnp.add, [_dot(part, sel) for part, sel in
                                      zip(_split(-c, FOX_BIAS_TERMS), _fox_key_bias_selectors())])
    slot = lax.broadcasted_iota(jnp.int32, (tm, FOX_PAIRS * LANES), 1) % LANES
    ones = (slot >= 2 * FOX_BIAS_TERMS) & (slot < 3 * FOX_BIAS_TERMS)
    bias = jnp.where(ones, 1.0, bias).astype(BF16)
    for p in range(FOX_PAIRS):
        kx_ref[:, p * FOX_KEY_LANES:p * FOX_KEY_LANES + LANES] = (
            k[:, p * LANES:(p + 1) * LANES].astype(BF16))
        kx_ref[:, p * FOX_KEY_LANES + LANES:(p + 1) * FOX_KEY_LANES] = (
            bias[:, p * LANES:(p + 1) * LANES])
    rz_ref[...] = _dot(h, w_ref[:, FOX_WIDTH + LANES:])


def _even_in(x2, g, w_pack, w_qv_t, fb, batch):
    n = x2.shape[0]
    tm = ROW_TILE
    row = lambda w: pl.BlockSpec((tm, w), lambda i: (i, 0))
    col = lambda h: pl.BlockSpec((h, tm), lambda i: (0, i))
    return pl.pallas_call(
        functools.partial(_even_in_kernel, tiles_per_batch=n // batch // tm),
        grid=(n // tm,),
        in_specs=[row(D_MODEL), _const_spec((1, D_MODEL)), _const_spec((D_MODEL, EVEN_PACKED)),
                  _const_spec((2 * FOX_WIDTH, D_MODEL)), _const_spec((1, LANES))],
        out_specs=[col(FOX_WIDTH), row(FOX_PAIRS * FOX_KEY_LANES), col(FOX_WIDTH),
                   col(N_FOX_HEADS), row(RWKV_IN)],
        out_shape=[jax.ShapeDtypeStruct((FOX_WIDTH, n), BF16),
                   jax.ShapeDtypeStruct((n, FOX_PAIRS * FOX_KEY_LANES), BF16),
                   jax.ShapeDtypeStruct((FOX_WIDTH, n), BF16),
                   jax.ShapeDtypeStruct((N_FOX_HEADS, n), F32),
                   jax.ShapeDtypeStruct((n, RWKV_IN), F32)],
        scratch_shapes=[pltpu.VMEM((8, LANES), F32)],
        compiler_params=_params("arbitrary"),
        name="even_in",
    )(x2, g, w_pack, w_qv_t, fb)


def _fox_kernel(qt_ref, kx_ref, vt_ref, ct_ref, o_ref):
    hp = pl.program_id(1)
    qi = pl.program_id(2)
    tq = qt_ref.shape[1]
    qt = qt_ref[...]
    row = lax.broadcasted_iota(jnp.int32, (LANES, tq), 0)
    head0 = row < HEAD_DIM
    zero = jnp.zeros_like(qt)

    def bias_rows(hh):
        c = ct_ref[pl.ds(2 * hp + hh, 1), :]
        terms = [part.astype(F32) for part in _split(c, FOX_BIAS_TERMS)]
        b = jnp.zeros((LANES, tq), F32)
        for t, term in enumerate(terms):
            b = jnp.where(row == 2 * FOX_BIAS_TERMS + t, term, b)
        own = (row >= FOX_BIAS_TERMS * hh) & (row < FOX_BIAS_TERMS * (hh + 1))
        return jnp.where(own, 1.0, b).astype(BF16)

    qxt = jnp.concatenate([
        jnp.concatenate([jnp.where(head0, qt, zero), bias_rows(0)], axis=0),
        jnp.concatenate([jnp.where(head0, zero, qt), bias_rows(1)], axis=0)], axis=1)
    tk = tq
    key_idx = lax.broadcasted_iota(jnp.int32, (tk, 2 * tq), 0)
    query_idx = lax.broadcasted_iota(jnp.int32, (tk, 2 * tq), 1) % tq
    causal = key_idx <= query_idx

    def step(j, carry, masked):
        m, l, acc = carry
        k0 = pl.multiple_of(j * tk, tk)
        s = _dot(kx_ref[pl.ds(k0, tk), :], qxt)
        if masked:
            s = jnp.where(causal, s, NEG_INF)
        m_new = jnp.maximum(m, jnp.max(s, axis=0, keepdims=True))
        p = jnp.exp2(s - m_new)
        alpha = jnp.exp2(m - m_new)
        l = alpha * l + jnp.sum(p, axis=0, keepdims=True)
        acc = alpha * acc + _dot(vt_ref[:, pl.ds(k0, tk)], p.astype(BF16))
        return m_new, l, acc

    init = (jnp.full((1, 2 * tq), NEG_INF, F32), jnp.zeros((1, 2 * tq), F32),
            jnp.zeros((LANES, 2 * tq), F32))
    carry = lax.fori_loop(0, qi, lambda j, c: step(j, c, False), init)
    _, l, acc = step(qi, carry, True)
    out = acc / l
    o_ref[...] = jnp.where(head0, out[:, :tq], out[:, tq:]).astype(BF16)


def _fox_attention(qt, kx, vt, ct, batch):
    n = qt.shape[1]
    t = n // batch
    tq = FOX_TILE
    nq = t // tq
    return pl.pallas_call(
        _fox_kernel,
        grid=(batch, FOX_PAIRS, nq),
        in_specs=[
            pl.BlockSpec((LANES, tq), lambda b, p, i: (p, b * nq + i)),
            pl.BlockSpec((t, FOX_KEY_LANES), lambda b, p, i: (b, p)),
            pl.BlockSpec((LANES, t), lambda b, p, i: (p, b)),
            pl.BlockSpec((N_FOX_HEADS, tq), lambda b, p, i: (0, b * nq + i)),
        ],
        out_specs=pl.BlockSpec((LANES, tq), lambda b, p, i: (p, b * nq + i)),
        out_shape=jax.ShapeDtypeStruct((FOX_WIDTH, n), BF16),
        compiler_params=_params("arbitrary", "arbitrary", "arbitrary"),
        name="fox_attention",
    )(qt, kx, vt, ct)


def _rwkv_kernel(rz_ref, mu_ref, w0_ref, w2_ref, a0_ref, a2_ref, g2_ref, kk_ref, ka_ref, rk_ref,
                 lnw_ref, lnb_ref, y_ref, state_ref, tail_ref, ybuf_ref):
    tc = rz_ref.shape[0]
    ch = RWKV_CHUNK
    gw = RWKV_GROUP
    n_groups = RWKV_WIDTH // gw

    @pl.when(pl.program_id(1) == 0)
    def _():
        state_ref[...] = jnp.zeros_like(state_ref)
        tail_ref[...] = jnp.zeros_like(tail_ref)

    z = rz_ref[...]
    rows = lax.broadcasted_iota(jnp.int32, z.shape, 0)
    z_prev = jnp.where(rows == 0, tail_ref[7:8, :], pltpu.roll(z, 1, axis=0))
    tail_ref[...] = z[tc - 8:tc, :]
    z = z + (z_prev - z) * mu_ref[...]
    r = z[:, 0:RWKV_WIDTH]
    k = z[:, RWKV_WIDTH:2 * RWKV_WIDTH]
    v = z[:, 2 * RWKV_WIDTH:3 * RWKV_WIDTH]
    lo = z[:, 3 * RWKV_WIDTH:3 * RWKV_WIDTH + LANES]
    g_lo = z[:, 3 * RWKV_WIDTH + LANES:]

    w = w0_ref[...] + _dot(jnp.tanh(lo).astype(BF16), w2_ref[...])
    w = -_softplus(-w) - 0.5
    log_decay = -jnp.exp(w)
    a = _sigmoid(a0_ref[...] + _dot(lo.astype(BF16), a2_ref[...]))
    gate = _dot(_sigmoid(g_lo).astype(BF16), g2_ref[...])

    br = lax.broadcasted_iota(jnp.int32, (gw, gw), 0) // HEAD_DIM
    bc = lax.broadcasted_iota(jnp.int32, (gw, gw), 1) // HEAD_DIM
    same_head = br == bc
    head_ones = jnp.where(same_head, 1.0, 0.0).astype(BF16)

    def seg_sum(t):
        return jnp.concatenate(
            [_dot_exact_rhs(t[:, g * gw:(g + 1) * gw], head_ones, terms=2)
             for g in range(n_groups)], axis=1)

    kk = k * kk_ref[...]
    kk = kk / jnp.maximum(jnp.sqrt(seg_sum(kk * kk)), 1e-12)
    k = k * (1.0 + (a - 1.0) * ka_ref[...])
    aa = -kk
    bb = kk * a
    bonus = seg_sum(r * k * rk_ref[...]) * v

    tr = lax.broadcasted_iota(jnp.int32, (ch, ch), 0)
    ts = lax.broadcasted_iota(jnp.int32, (ch, ch), 1)
    tri_incl = jnp.where(tr >= ts, 1.0, 0.0).astype(BF16)
    gr = lax.broadcasted_iota(jnp.int32, (ch, gw), 0)
    gs = lax.broadcasted_iota(jnp.int32, (ch, gw), 1) % ch
    lower = gr >= gs
    strict = gr > gs
    eye = jnp.where(gr == gs, 1.0, 0.0)

    def block_diag(t):
        tiled = jnp.concatenate([t] * (gw // ch), axis=0)
        return jnp.where(same_head, tiled, 0.0).astype(BF16)

    stack = lambda x, y: jnp.concatenate([x, y], axis=0).astype(BF16)
    n_chunks = tc // ch
    chains = [(c, grp) for c in range(n_chunks) for grp in range(n_groups)]
    cs = {}
    for c in range(n_chunks):
        sl = slice(c * ch, (c + 1) * ch)
        ld = log_decay[sl]
        cum = _dot_exact_lhs(tri_incl, ld)
        cum_end = cum[ch - 1:ch, :]
        p_inv = jnp.exp(-cum)
        p_rest = jnp.exp(cum_end - cum)
        cs[c] = dict(p_end=jnp.exp(cum_end), a_t=aa[sl] * jnp.exp(cum - ld),
                     r_t=r[sl] * jnp.exp(cum), b_t=bb[sl] * p_inv, k_t=k[sl] * p_inv,
                     b_h=bb[sl] * p_rest, k_h=k[sl] * p_rest, v=v[sl])
    st = {}
    for c, grp in chains:
        gl = slice(grp * gw, (grp + 1) * gw)
        d = {name: val[:, gl] for name, val in cs[c].items()}
        ar = stack(d["a_t"], d["r_t"])
        sb = _dot_nt(ar, block_diag(d["b_t"]))
        sk = _dot_nt(ar, block_diag(d["k_t"]))
        d["a_ab"] = jnp.where(strict, sb[:ch], 0.0)
        d["a_rb"] = jnp.where(lower, sb[ch:], 0.0)
        a_ak = jnp.where(strict, sk[:ch], 0.0)
        a_rk = jnp.where(lower, sk[ch:], 0.0)
        av = _dot(stack(a_ak, a_rk), block_diag(d["v"]))
        d["av"], d["rv"] = av[:ch], av[ch:]
        d["inv"] = eye + d["a_ab"]
        d["power"] = d["a_ab"]
        st[c, grp] = d
    levels = ch.bit_length() - 1
    for lvl in range(levels):
        for key in chains:
            d = st[key]
            rhs = block_diag(d["power"])
            if lvl == 0:
                d["power"] = _dot(d["power"].astype(BF16), rhs)
            elif lvl < levels - 1:
                both = _dot(stack(d["power"], d["inv"]), rhs)
                d["power"] = both[:ch]
                d["inv"] = d["inv"] + both[ch:]
            else:
                d["inv"] = d["inv"] + _dot(d["inv"].astype(BF16), rhs)
    for key in chains:
        d = st[key]
        inv = d["inv"].astype(BF16)
        d["ta"] = _dot(inv, block_diag(d["a_t"]))
        d["u0"] = _dot(inv, block_diag(d["av"]))
    for key in chains:
        d = st[key]
        a_rb = d["a_rb"].astype(BF16)
        d["query"] = d["r_t"] + _dot(a_rb, block_diag(d["ta"]))
        d["y0"] = d["rv"] + _dot(a_rb, block_diag(d["u0"]))
        d["mix"] = jnp.where(same_head, _dot_tn(d["ta"].astype(BF16), d["b_h"].astype(BF16)), 0.0)
        d["add"] = jnp.where(same_head, _dot_tn(stack(d["u0"], d["v"]),
                                                stack(d["b_h"], d["k_h"])), 0.0)
    for grp in range(n_groups):
        gl = slice(grp * gw, (grp + 1) * gw)
        g_state = state_ref[grp]
        for c in range(n_chunks):
            d = st[c, grp]
            g_bf = g_state.astype(BF16)
            ybuf_ref[c * ch:(c + 1) * ch, gl] = _dot_nt(d["query"].astype(BF16), g_bf) + d["y0"]
            g_state = g_state * d["p_end"] + _dot(g_bf, d["mix"].astype(BF16)) + d["add"]
        state_ref[grp] = g_state

    y = ybuf_ref[...]
    inv_n = 1.0 / HEAD_DIM
    mean = seg_sum(y) * inv_n
    d = y - mean
    var = seg_sum(d * d) * inv_n
    yn = d * lax.rsqrt(var + GN_EPS) * lnw_ref[...] + lnb_ref[...]
    y_ref[...] = ((yn + bonus) * gate).astype(BF16)


def _rwkv_mix(rz, mu, w0, w2p, a0, a2p, g2, k_k, k_a, r_k, ln_w, ln_b, batch):
    n = rz.shape[0]
    t = n // batch
    tc = RWKV_TILE
    nt = t // tc
    vec = lambda w: _const_spec((1, w))
    return pl.pallas_call(
        _rwkv_kernel,
        grid=(batch, nt),
        in_specs=[pl.BlockSpec((tc, RWKV_IN), lambda b, i: (b * nt + i, 0)),
                  vec(RWKV_IN), vec(RWKV_WIDTH), _const_spec((LANES, RWKV_WIDTH)),
                  vec(RWKV_WIDTH), _const_spec((LANES, RWKV_WIDTH)),
                  _const_spec((GATE_LORA, RWKV_WIDTH)),
                  vec(RWKV_WIDTH), vec(RWKV_WIDTH), vec(RWKV_WIDTH), vec(RWKV_WIDTH),
                  vec(RWKV_WIDTH)],
        out_specs=pl.BlockSpec((tc, RWKV_WIDTH), lambda b, i: (b * nt + i, 0)),
        out_shape=jax.ShapeDtypeStruct((n, RWKV_WIDTH), BF16),
        scratch_shapes=[pltpu.VMEM((RWKV_WIDTH // RWKV_GROUP, RWKV_GROUP, RWKV_GROUP), F32),
                        pltpu.VMEM((8, RWKV_IN), F32),
                        pltpu.VMEM((tc, RWKV_WIDTH), F32)],
        compiler_params=_params("arbitrary", "arbitrary"),
        name="rwkv_mix",
    )(rz, mu, w0, w2p, a0, a2p, g2, k_k, k_a, r_k, ln_w, ln_b)


def _rope_table_kernel(pos_ref, freq_ref, cos_ref, sin_up_ref, sin_dn_ref):
    ang = pos_ref[...] * freq_ref[...]
    c = jnp.cos(ang)
    s = jnp.sin(ang)
    d = lax.broadcasted_iota(jnp.int32, ang.shape, 1) % HEAD_DIM
    half = ROPE_DIMS // 2
    cos_ref[...] = jnp.where(d < ROPE_DIMS, c, 1.0)
    sin_up_ref[...] = jnp.where(d < half, -s, 0.0)
    sin_dn_ref[...] = jnp.where((d >= half) & (d < ROPE_DIMS), s, 0.0)


def _rope_tables(positions):
    n = positions.size
    tm = 2048
    pos = jnp.broadcast_to(positions.reshape(n, 1).astype(F32), (n, LANES))
    half = ROPE_DIMS // 2
    inv_freq = ROPE_THETA ** (-jnp.arange(0, ROPE_DIMS, 2, dtype=F32) / ROPE_DIMS)
    d = jnp.arange(LANES) % HEAD_DIM
    freq = jnp.where(d < ROPE_DIMS, inv_freq[d % half], 0.0).reshape(1, LANES)
    row = pl.BlockSpec((tm, LANES), lambda i: (i, 0))
    return pl.pallas_call(
        _rope_table_kernel,
        grid=(n // tm,),
        in_specs=[row, _const_spec((1, LANES))],
        out_specs=[row, row, row],
        out_shape=[jax.ShapeDtypeStruct((n, LANES), F32)] * 3,
        compiler_params=_params("arbitrary"),
        name="rope_tables",
    )(pos, freq)


def _odd_in_kernel(x_ref, g_ref, w_ref, cos_ref, up_ref, dn_ref, *refs):
    outs, perm_ref = refs[:-1], refs[-1]
    tm = x_ref.shape[0]
    h = _rms_norm(x_ref[...], g_ref[...]).astype(BF16)
    reps = DIL_WIDTH // LANES
    cos = jnp.concatenate([cos_ref[...]] * reps, axis=1)
    up = jnp.concatenate([up_ref[...]] * reps, axis=1)
    dn = jnp.concatenate([dn_ref[...]] * reps, axis=1)
    half = ROPE_DIMS // 2

    def rotary(t):
        return (t * cos + pltpu.roll(t, DIL_WIDTH - half, axis=1) * up
                + pltpu.roll(t, half, axis=1) * dn)

    def emit(idx, val):
        outs[idx][...] = val.astype(BF16)
        for c in range(reps):
            perm_ref[c] = val[:, c * LANES:(c + 1) * LANES]
        for gi, (_, d) in enumerate(DILATED_GROUPS[1:], start=1):
            ref = outs[gi * 3 + idx]
            for rho in range(d):
                for c in range(reps):
                    ref[0, rho, :, c * LANES:(c + 1) * LANES] = perm_ref[
                        c, pl.ds(rho, tm // d, stride=d), :].astype(BF16)

    emit(0, rotary(_dot(h, w_ref[:, 0:DIL_WIDTH])) * HEAD_DIM ** -0.5)
    emit(1, rotary(_dot(h, w_ref[:, DIL_WIDTH:2 * DIL_WIDTH])))
    emit(2, _dot(h, w_ref[:, 2 * DIL_WIDTH:]))


def _residue_spec(d, tm, width, tiles_per_batch):
    return pl.BlockSpec((1, d, tm // d, width),
                        lambda i: (i // tiles_per_batch, 0, i % tiles_per_batch, 0))


def _odd_in(x2, g, w, cos, up, dn, batch):
    n = x2.shape[0]
    t = n // batch
    tm = ROW_TILE
    row = lambda w_: pl.BlockSpec((tm, w_), lambda i: (i, 0))
    out_specs = [row(DIL_WIDTH)] * 3
    out_shape = [jax.ShapeDtypeStruct((n, DIL_WIDTH), BF16)] * 3
    for _, d in DILATED_GROUPS[1:]:
        out_specs += [_residue_spec(d, tm, DIL_WIDTH, t // tm)] * 3
        out_shape += [jax.ShapeDtypeStruct((batch, d, t // d, DIL_WIDTH), BF16)] * 3
    return pl.pallas_call(
        _odd_in_kernel,
        grid=(n // tm,),
        in_specs=[row(D_MODEL), _const_spec((1, D_MODEL)), _const_spec((D_MODEL, 3 * DIL_WIDTH)),
                  row(LANES), row(LANES), row(LANES)],
        out_specs=out_specs,
        out_shape=out_shape,
        scratch_shapes=[pltpu.VMEM((DIL_WIDTH // LANES, tm, LANES), F32)],
        compiler_params=_params("arbitrary"),
        name="odd_in",
    )(x2, g, w, cos, up, dn)


def _dilated_kernel(q_ref, kc_ref, kp_ref, vc_ref, vp_ref, o_ref, lse_ref):
    blk = pl.program_id(2)
    sp = DIL_SPAN
    qrows = q_ref.shape[2]
    lane = lax.broadcasted_iota(jnp.int32, (sp, LANES), 1)
    head0 = lane < HEAD_DIM
    qi = lax.broadcasted_iota(jnp.int32, (2 * sp, 2 * sp), 0) % sp
    ki = lax.broadcasted_iota(jnp.int32, (2 * sp, 2 * sp), 1)
    band = (ki >= qi) & (ki <= qi + sp)
    bias = jnp.where(band, 0.0, NEG_INF)
    bias_first = jnp.where(band & (ki >= jnp.where(blk > 0, 0, sp)), 0.0, NEG_INF)
    n_steps = DIL_WIDTH // LANES // DIL_PAIRS_PER_STEP

    for sub in range(qrows // sp):
        r0 = sub * sp

        def step(g, lse_acc, r0=r0, sub=sub):
            for pp in range(DIL_PAIRS_PER_STEP):
                p = g * DIL_PAIRS_PER_STEP + pp
                cols = pl.ds(pl.multiple_of(p * LANES, LANES), LANES)
                q = q_ref[0, 0, r0:r0 + sp, cols]
                if sub == 0:
                    kb = jnp.concatenate([kp_ref[0, 0, qrows - sp:qrows, cols],
                                          kc_ref[0, 0, 0:sp, cols]], axis=0)
                    vb = jnp.concatenate([vp_ref[0, 0, qrows - sp:qrows, cols],
                                          vc_ref[0, 0, 0:sp, cols]], axis=0)
                else:
                    kb = kc_ref[0, 0, r0 - sp:r0 + sp, cols]
                    vb = vc_ref[0, 0, r0 - sp:r0 + sp, cols]
                zero = jnp.zeros_like(q)
                q2 = jnp.concatenate([jnp.where(head0, q, zero), jnp.where(head0, zero, q)], axis=0)
                s = _dot_nt(q2, kb) + (bias_first if sub == 0 else bias)
                m = jnp.max(s, axis=1, keepdims=True)
                e = jnp.exp(s - m)
                den = jnp.sum(e, axis=1, keepdims=True)
                o2 = _dot(e.astype(BF16), vb) * (1.0 / den)
                lse = m + jnp.log(den)
                o_ref[0, 0, r0:r0 + sp, cols] = jnp.where(head0, o2[:sp], o2[sp:]).astype(o_ref.dtype)
                lse_acc = jnp.where(lane == 2 * p, lse[:sp],
                                    jnp.where(lane == 2 * p + 1, lse[sp:], lse_acc))
            return lse_acc

        lse_all = lax.fori_loop(0, n_steps, step, jnp.zeros((sp, LANES), F32))
        lse_ref[0, 0, r0:r0 + sp, :] = lse_all


def _dilated_branch(q, k, v, batch):
    _, dilation, length, _ = q.shape
    qrows = min(DIL_QROWS, length)
    nb = length // qrows
    cur = pl.BlockSpec((1, 1, qrows, DIL_WIDTH), lambda b, r, i: (b, r, i, 0))
    prev = pl.BlockSpec((1, 1, qrows, DIL_WIDTH), lambda b, r, i: (b, r, jnp.maximum(i - 1, 0), 0))
    return pl.pallas_call(
        _dilated_kernel,
        grid=(batch, dilation, nb),
        in_specs=[cur, cur, prev, cur, prev],
        out_specs=[cur, pl.BlockSpec((1, 1, qrows, LANES), lambda b, r, i: (b, r, i, 0))],
        out_shape=[jax.ShapeDtypeStruct((batch, dilation, length, DIL_WIDTH), BF16),
                   jax.ShapeDtypeStruct((batch, dilation, length, LANES), F32)],
        compiler_params=_params("arbitrary", "arbitrary", "arbitrary"),
        name=f"dilated_d{dilation}",
    )(q, k, k, v, v)


def _gelu_tanh(x):
    return 0.5 * x * (1.0 + jnp.tanh(0.7978845608028654 * (x + 0.044715 * x * x * x)))


def _finish_layer(m, x_ref, gmp_ref, gfp_ref, gfo_ref, wup_ref, cw_ref, cb_ref, wdn_ref, out_ref,
                  tail_ref, act_ref, tiles_per_batch):
    i = pl.program_id(0)
    tm = x_ref.shape[0]
    x1 = x_ref[...] + _rms_norm(m, gmp_ref[...])
    h = _rms_norm(x1, gfp_ref[...]).astype(BF16)

    @pl.when(i % tiles_per_batch == 0)
    def _():
        tail_ref[...] = jnp.zeros_like(tail_ref)

    ck = FFN_CHUNK
    rows = lax.broadcasted_iota(jnp.int32, (tm, ck), 0)

    def conv(col):
        u = _dot(h, wup_ref[:, col:col + ck])
        t1 = tail_ref[7:8, col:col + ck]
        t2 = tail_ref[6:7, col:col + ck]
        u1 = jnp.where(rows == 0, t1, pltpu.roll(u, 1, axis=0))
        u2 = jnp.where(rows == 0, t2, jnp.where(rows == 1, t1, pltpu.roll(u, 2, axis=0)))
        tail_ref[:, col:col + ck] = u[tm - 8:tm, :]
        return (cb_ref[:, col:col + ck] + cw_ref[2:3, col:col + ck] * u
                + cw_ref[1:2, col:col + ck] * u1 + cw_ref[0:1, col:col + ck] * u2)

    for c in range(D_FF // ck):
        gate = conv(c * ck)
        val = conv(D_FF + c * ck)
        act_ref[:, c * ck:(c + 1) * ck] = (_gelu_tanh(gate) * val).astype(BF16)
    f = _dot(act_ref[...], wdn_ref[...])
    out_ref[...] = x1 + _rms_norm(f, gfo_ref[...])


def _even_post_kernel(yft_ref, yr_ref, x_ref, wo_ref, *rest, tiles_per_batch):
    m = _dot_tn(yft_ref[...], wo_ref[0:FOX_WIDTH, :]) + _dot(yr_ref[...], wo_ref[FOX_WIDTH:, :])
    _finish_layer(m, x_ref, *rest, tiles_per_batch=tiles_per_batch)


def _time_order(src_ref, perm_ref):
    _, d, rows, width = src_ref.shape
    if d == 1:
        return src_ref[0, 0].astype(F32)
    blocks = width // LANES
    for rho in range(d):
        for c in range(blocks):
            perm_ref[c, pl.ds(rho, rows, stride=d), :] = src_ref[
                0, rho, :, c * LANES:(c + 1) * LANES].astype(F32)
    return jnp.concatenate([perm_ref[c] for c in range(blocks)], axis=1)


def _odd_post_kernel(*refs, tiles_per_batch):
    ng = len(DILATED_GROUPS)
    o_refs, l_refs = refs[:ng], refs[ng:2 * ng]
    x_ref, wo_ref = refs[2 * ng:2 * ng + 2]
    rest, (perm_o_ref, perm_l_ref) = refs[2 * ng + 2:-2], refs[-2:]
    lses = [_time_order(l_ref, perm_l_ref) for l_ref in l_refs]
    m = functools.reduce(jnp.maximum, lses)
    es = [jnp.exp(l - m) for l in lses]
    inv = 1.0 / functools.reduce(jnp.add, es)
    hr = lax.broadcasted_iota(jnp.int32, (LANES, DIL_WIDTH), 0)
    hc = lax.broadcasted_iota(jnp.int32, (LANES, DIL_WIDTH), 1) // HEAD_DIM
    expand = jnp.where(hr == hc, 1.0, 0.0).astype(BF16)
    o = None
    for e, o_ref in zip(es, o_refs):
        term = _dot_exact_rhs(e * inv, expand) * _time_order(o_ref, perm_o_ref)
        o = term if o is None else o + term
    _finish_layer(_dot(o.astype(BF16), wo_ref[...]), x_ref, *rest, tiles_per_batch=tiles_per_batch)


def _post_ffn(mixer_outs, x2, wo, g_mix_post, g_ffn_pre, g_ffn_post, w_up, conv_w, conv_b, w_dn,
              batch, even):
    n = x2.shape[0]
    tm = ROW_TILE
    tpb = n // batch // tm
    row = lambda w: pl.BlockSpec((tm, w), lambda i: (i, 0))
    vec = _const_spec((1, D_MODEL))
    def spec(a):
        if a.ndim == 4:
            return _residue_spec(a.shape[1], tm, a.shape[3], tpb)
        if a.shape[0] == n:
            return row(a.shape[1])
        return pl.BlockSpec((a.shape[0], tm), lambda i: (0, i))

    mixer_specs = [spec(a) for a in mixer_outs]
    body = _even_post_kernel if even else _odd_post_kernel
    scratch = [pltpu.VMEM((8, 2 * D_FF), F32), pltpu.VMEM((tm, D_FF), BF16)]
    if not even:
        scratch += [pltpu.VMEM((DIL_WIDTH // LANES, tm, LANES), F32), pltpu.VMEM((1, tm, LANES), F32)]
    return pl.pallas_call(
        functools.partial(body, tiles_per_batch=tpb),
        grid=(n // tm,),
        in_specs=mixer_specs + [row(D_MODEL), _const_spec((D_MODEL, D_MODEL)), vec, vec, vec,
                                _const_spec((D_MODEL, 2 * D_FF)), _const_spec((8, 2 * D_FF)),
                                _const_spec((1, 2 * D_FF)), _const_spec((D_FF, D_MODEL))],
        out_specs=row(D_MODEL),
        out_shape=jax.ShapeDtypeStruct((n, D_MODEL), F32),
        scratch_shapes=scratch,
        compiler_params=_params("arbitrary"),
        name="post_ffn_even" if even else "post_ffn_odd",
    )(*mixer_outs, x2, wo, g_mix_post, g_ffn_pre, g_ffn_post, w_up, conv_w, conv_b, w_dn)


def _pad_rows(a, rows, before=0):
    return jnp.pad(a, ((before, rows - a.shape[0] - before), (0, 0)))


def kernel(x, positions, norm_mix_pre, norm_mix_post, norm_ffn_pre, norm_ffn_post, even_w_in, fox_forget_bias, rwkv_mu, rwkv_w0, rwkv_w2, rwkv_a0, rwkv_a2, rwkv_g2, rwkv_k_k, rwkv_k_a, rwkv_r_k, rwkv_ln_w, rwkv_ln_b, even_w_out, odd_w_in, odd_w_out, ffn_w_up, ffn_conv_w, ffn_conv_b, ffn_w_down):
    batch, seq, _ = x.shape
    n = batch * seq
    depth = norm_mix_pre.shape[0]
    x2 = x.reshape(n, D_MODEL)
    rope = None
    vec = lambda a: a.reshape(1, -1)
    for layer in range(depth):
        i = layer // 2
        g_pre = vec(norm_mix_pre[layer])
        if layer % 2 == 0:
            w_in = even_w_in[i]
            f0 = 3 * FOX_WIDTH
            w_pack = jnp.concatenate(
                [w_in[:, FOX_WIDTH:2 * FOX_WIDTH],
                 jnp.pad(w_in[:, f0:FOX_IN], ((0, 0), (0, LANES - N_FOX_HEADS))),
                 w_in[:, FOX_IN:]], axis=1).astype(BF16)
            w_qv_t = jnp.concatenate([w_in[:, :FOX_WIDTH], w_in[:, 2 * FOX_WIDTH:f0]],
                                     axis=1).T.astype(BF16)
            fb = jnp.pad(fox_forget_bias[i], (0, LANES - N_FOX_HEADS)).reshape(1, LANES)
            qt, kx, vt, ct, rz = _even_in(x2, g_pre, w_pack, w_qv_t, fb, batch)
            y_fox = _fox_attention(qt, kx, vt, ct, batch)
            w2p = _pad_rows(rwkv_w2[i], LANES).astype(BF16)
            a2p = _pad_rows(rwkv_a2[i], LANES, before=DECAY_LORA).astype(BF16)
            y_rwkv = _rwkv_mix(rz, vec(rwkv_mu[i]), vec(rwkv_w0[i]), w2p, vec(rwkv_a0[i]), a2p,
                               rwkv_g2[i].astype(BF16), vec(rwkv_k_k[i]), vec(rwkv_k_a[i]),
                               vec(rwkv_r_k[i]), vec(rwkv_ln_w[i]), vec(rwkv_ln_b[i]), batch)
            mixer_outs = [y_fox, y_rwkv]
            w_out = even_w_out[i]
        else:
            if rope is None:
                rope = _rope_tables(positions)
            qkv = _odd_in(x2, g_pre, odd_w_in[i].astype(BF16), *rope, batch)
            outs, lses = [], []
            for gi, (_, d) in enumerate(DILATED_GROUPS):
                q, k, v = [a.reshape(batch, d, -1, DIL_WIDTH) for a in qkv[3 * gi:3 * gi + 3]]
                o, lse = _dilated_branch(q, k, v, batch)
                outs.append(o)
                lses.append(lse)
            mixer_outs = outs + lses
            w_out = odd_w_out[i]
        x2 = _post_ffn(mixer_outs, x2, w_out.astype(BF16), vec(norm_mix_post[layer]),
                       vec(norm_ffn_pre[layer]), vec(norm_ffn_post[layer]),
                       ffn_w_up[layer].astype(BF16), _pad_rows(ffn_conv_w[layer], 8),
                       vec(ffn_conv_b[layer]), ffn_w_down[layer].astype(BF16), batch,
                       even=layer % 2 == 0)
    return x2.reshape(batch, seq, D_MODEL)
```

```python
import functools

import jax
import jax.numpy as jnp
from jax import lax
from jax.experimental import pallas as pl
from jax.experimental.pallas import tpu as pltpu

F32 = jnp.float32
BF16 = jnp.bfloat16

D_MODEL = 1024
HEAD_DIM = 64
N_FOX_HEADS = 8
FOX_WIDTH = 512
RWKV_WIDTH = 512
DECAY_LORA = 64
ICLR_LORA = 64
GATE_LORA = 128
RWKV_IN = 3 * RWKV_WIDTH + DECAY_LORA + ICLR_LORA + GATE_LORA
FOX_IN = 3 * FOX_WIDTH + N_FOX_HEADS
N_DIL_HEADS = 16
DIL_WIDTH = 1024
DILATED_GROUPS = ((128, 1), (512, 4), (2048, 16))
DIL_SPAN = 128
ROPE_THETA = 500000.0
ROPE_DIMS = 16
D_FF = 2816
RMS_EPS = 1e-6
GN_EPS = 64e-5
NEG_INF = -1e30
LOG2_E = 1.4426950408889634

LANES = 128
V7X_VMEM_LIMIT = 56 * 1024 * 1024

ROW_TILE = 512
FOX_TILE = 1024
RWKV_TILE = 256
RWKV_CHUNK = 64
RWKV_GROUP = 256
DIL_QROWS = 512
DIL_PAIRS_PER_STEP = 8
FFN_CHUNK = 256


def _dot(a, b):
    return jnp.dot(a, b, preferred_element_type=F32)


def _dot_nt(a, b):
    return lax.dot_general(a, b, (((1,), (1,)), ((), ())), preferred_element_type=F32)


def _dot_tn(a, b):
    return lax.dot_general(a, b, (((0,), (0,)), ((), ())), preferred_element_type=F32)


def _split(x, terms):
    parts = []
    for _ in range(terms):
        part = x.astype(BF16)
        parts.append(part)
        x = x - part.astype(F32)
    return parts


def _dot_exact_lhs(sel, x, terms=3):
    return functools.reduce(jnp.add, [_dot(sel, part) for part in _split(x, terms)])


def _dot_exact_rhs(x, sel, terms=3):
    return functools.reduce(jnp.add, [_dot(part, sel) for part in _split(x, terms)])


def _rms_norm(x, g):
    return x * lax.rsqrt(jnp.mean(x * x, axis=-1, keepdims=True) + RMS_EPS) * g


def _softplus(x):
    return jnp.maximum(x, 0.0) + jnp.log1p(jnp.exp(-jnp.abs(x)))


def _sigmoid(x):
    return 1.0 / (1.0 + jnp.exp(-x))


def _const_spec(shape):
    nd = len(shape)
    return pl.BlockSpec(shape, lambda *_: (0,) * nd, pipeline_mode=pl.Buffered(1))


def _params(*sem):
    return pltpu.CompilerParams(dimension_semantics=sem, vmem_limit_bytes=V7X_VMEM_LIMIT)


EVEN_PACKED = FOX_WIDTH + LANES + RWKV_IN

FOX_PAIRS = FOX_WIDTH // LANES
FOX_KEY_LANES = 2 * LANES
FOX_BIAS_TERMS = 3


def _fox_key_bias_selectors():
    r = lax.broadcasted_iota(jnp.int32, (LANES, FOX_PAIRS * LANES), 0)
    cidx = lax.broadcasted_iota(jnp.int32, (LANES, FOX_PAIRS * LANES), 1)
    pair, slot = cidx // LANES, cidx % LANES
    return [jnp.where((r // 2 == pair) & (r < N_FOX_HEADS)
                      & (slot == FOX_BIAS_TERMS * (r % 2) + t), 1.0, 0.0).astype(BF16)
            for t in range(FOX_BIAS_TERMS)]


def _even_in_kernel(x_ref, g_ref, w_ref, wt_ref, fb_ref, qt_ref, kx_ref, vt_ref, ct_ref, rz_ref,
                    carry_ref, *, tiles_per_batch):
    i = pl.program_id(0)
    tm = x_ref.shape[0]
    h = _rms_norm(x_ref[...], g_ref[...]).astype(BF16)
    qt_ref[...] = (_dot_nt(wt_ref[0:FOX_WIDTH, :], h) * (LOG2_E * HEAD_DIM ** -0.5)).astype(BF16)
    vt_ref[...] = _dot_nt(wt_ref[FOX_WIDTH:, :], h).astype(BF16)
    f = _dot(h, w_ref[:, FOX_WIDTH:FOX_WIDTH + LANES]) + fb_ref[...]
    log2_f = -_softplus(-f) * LOG2_E

    @pl.when(i % tiles_per_batch == 0)
    def _():
        carry_ref[...] = jnp.zeros_like(carry_ref)

    row = lax.broadcasted_iota(jnp.int32, (tm, tm), 0)
    col = lax.broadcasted_iota(jnp.int32, (tm, tm), 1)
    tri = jnp.where(row >= col, 1.0, 0.0).astype(BF16)
    c = _dot_exact_lhs(tri, log2_f) + carry_ref[0:1, :]
    carry_ref[...] = jnp.broadcast_to(c[tm - 1:tm, :], carry_ref.shape)
    ct_ref[...] = jnp.transpose(c)[0:N_FOX_HEADS, :]

    k = _dot(h, w_ref[:, 0:FOX_WIDTH])
    bias = functools.reduce(jnp.add, [_dot(part, sel) for part, sel in
                                      zip(_split(-c, FOX_BIAS_TERMS), _fox_key_bias_selectors())])
    slot = lax.broadcasted_iota(jnp.int32, (tm, FOX_PAIRS * LANES), 1) % LANES
    ones = (slot >= 2 * FOX_BIAS_TERMS) & (slot < 3 * FOX_BIAS_TERMS)
    bias = jnp.where(ones, 1.0, bias).astype(BF16)
    for p in range(FOX_PAIRS):
        kx_ref[:, p * FOX_KEY_LANES:p * FOX_KEY_LANES + LANES] = (
            k[:, p * LANES:(p + 1) * LANES].astype(BF16))
        kx_ref[:, p * FOX_KEY_LANES + LANES:(p + 1) * FOX_KEY_LANES] = (
            bias[:, p * LANES:(p + 1) * LANES])
    rz_ref[...] = _dot(h, w_ref[:, FOX_WIDTH + LANES:])


def _even_in(x2, g, w_pack, w_qv_t, fb, batch):
    n = x2.shape[0]
    tm = ROW_TILE
    row = lambda w: pl.BlockSpec((tm, w), lambda i: (i, 0))
    col = lambda h: pl.BlockSpec((h, tm), lambda i: (0, i))
    return pl.pallas_call(
        functools.partial(_even_in_kernel, tiles_per_batch=n // batch // tm),
        grid=(n // tm,),
        in_specs=[row(D_MODEL), _const_spec((1, D_MODEL)), _const_spec((D_MODEL, EVEN_PACKED)),
                  _const_spec((2 * FOX_WIDTH, D_MODEL)), _const_spec((1, LANES))],
        out_specs=[col(FOX_WIDTH), row(FOX_PAIRS * FOX_KEY_LANES), col(FOX_WIDTH),
                   col(N_FOX_HEADS), row(RWKV_IN)],
        out_shape=[jax.ShapeDtypeStruct((FOX_WIDTH, n), BF16),
                   jax.ShapeDtypeStruct((n, FOX_PAIRS * FOX_KEY_LANES), BF16),
                   jax.ShapeDtypeStruct((FOX_WIDTH, n), BF16),
                   jax.ShapeDtypeStruct((N_FOX_HEADS, n), F32),
                   jax.ShapeDtypeStruct((n, RWKV_IN), F32)],
        scratch_shapes=[pltpu.VMEM((8, LANES), F32)],
        compiler_params=_params("arbitrary"),
        name="even_in",
    )(x2, g, w_pack, w_qv_t, fb)


def _fox_kernel(qt_ref, kx_ref, vt_ref, ct_ref, o_ref):
    hp = pl.program_id(1)
    qi = pl.program_id(2)
    tq = qt_ref.shape[1]
    qt = qt_ref[...]
    row = lax.broadcasted_iota(jnp.int32, (LANES, tq), 0)
    head0 = row < HEAD_DIM
    zero = jnp.zeros_like(qt)

    def bias_rows(hh):
        c = ct_ref[pl.ds(2 * hp + hh, 1), :]
        terms = [part.astype(F32) for part in _split(c, FOX_BIAS_TERMS)]
        b = jnp.zeros((LANES, tq), F32)
        for t, term in enumerate(terms):
            b = jnp.where(row == 2 * FOX_BIAS_TERMS + t, term, b)
        own = (row >= FOX_BIAS_TERMS * hh) & (row < FOX_BIAS_TERMS * (hh + 1))
        return jnp.where(own, 1.0, b).astype(BF16)

    qxt = jnp.concatenate([
        jnp.concatenate([jnp.where(head0, qt, zero), bias_rows(0)], axis=0),
        jnp.concatenate([jnp.where(head0, zero, qt), bias_rows(1)], axis=0)], axis=1)
    tk = tq
    key_idx = lax.broadcasted_iota(jnp.int32, (tk, 2 * tq), 0)
    query_idx = lax.broadcasted_iota(jnp.int32, (tk, 2 * tq), 1) % tq
    causal = key_idx <= query_idx

    def step(j, carry, masked):
        m, l, acc = carry
        k0 = pl.multiple_of(j * tk, tk)
        s = _dot(kx_ref[pl.ds(k0, tk), :], qxt)
        if masked:
            s = jnp.where(causal, s, NEG_INF)
        m_new = jnp.maximum(m, jnp.max(s, axis=0, keepdims=True))
        p = jnp.exp2(s - m_new)
        alpha = jnp.exp2(m - m_new)
        l = alpha * l + jnp.sum(p, axis=0, keepdims=True)
        acc = alpha * acc + _dot(vt_ref[:, pl.ds(k0, tk)], p.astype(BF16))
        return m_new, l, acc

    init = (jnp.full((1, 2 * tq), NEG_INF, F32), jnp.zeros((1, 2 * tq), F32),
            jnp.zeros((LANES, 2 * tq), F32))
    carry = lax.fori_loop(0, qi, lambda j, c: step(j, c, False), init)
    _, l, acc = step(qi, carry, True)
    out = acc / l
    o_ref[...] = jnp.where(head0, out[:, :tq], out[:, tq:]).astype(BF16)


def _fox_attention(qt, kx, vt, ct, batch):
    n = qt.shape[1]
    t = n // batch
    tq = FOX_TILE
    nq = t // tq
    return pl.pallas_call(
        _fox_kernel,
        grid=(batch, FOX_PAIRS, nq),
        in_specs=[
            pl.BlockSpec((LANES, tq), lambda b, p, i: (p, b * nq + i)),
            pl.BlockSpec((t, FOX_KEY_LANES), lambda b, p, i: (b, p)),
            pl.BlockSpec((LANES, t), lambda b, p, i: (p, b)),
            pl.BlockSpec((N_FOX_HEADS, tq), lambda b, p, i: (0, b * nq + i)),
        ],
        out_specs=pl.BlockSpec((LANES, tq), lambda b, p, i: (p, b * nq + i)),
        out_shape=jax.ShapeDtypeStruct((FOX_WIDTH, n), BF16),
        compiler_params=_params("arbitrary", "arbitrary", "arbitrary"),
        name="fox_attention",
    )(qt, kx, vt, ct)


def _rwkv_kernel(rz_ref, mu_ref, w0_ref, w2_ref, a0_ref, a2_ref, g2_ref, kk_ref, ka_ref, rk_ref,
                 lnw_ref, lnb_ref, y_ref, state_ref, tail_ref, ybuf_ref):
    tc = rz_ref.shape[0]
    ch = RWKV_CHUNK
    gw = RWKV_GROUP
    n_groups = RWKV_WIDTH // gw

    @pl.when(pl.program_id(1) == 0)
    def _():
        state_ref[...] = jnp.zeros_like(state_ref)
        tail_ref[...] = jnp.zeros_like(tail_ref)

    z = rz_ref[...]
    rows = lax.broadcasted_iota(jnp.int32, z.shape, 0)
    z_prev = jnp.where(rows == 0, tail_ref[7:8, :], pltpu.roll(z, 1, axis=0))
    tail_ref[...] = z[tc - 8:tc, :]
    z = z + (z_prev - z) * mu_ref[...]
    r = z[:, 0:RWKV_WIDTH]
    k = z[:, RWKV_WIDTH:2 * RWKV_WIDTH]
    v = z[:, 2 * RWKV_WIDTH:3 * RWKV_WIDTH]
    lo = z[:, 3 * RWKV_WIDTH:3 * RWKV_WIDTH + LANES]
    g_lo = z[:, 3 * RWKV_WIDTH + LANES:]

    w = w0_ref[...] + _dot(jnp.tanh(lo).astype(BF16), w2_ref[...])
    w = -_softplus(-w) - 0.5
    log_decay = -jnp.exp(w)
    a = _sigmoid(a0_ref[...] + _dot(lo.astype(BF16), a2_ref[...]))
    gate = _dot(_sigmoid(g_lo).astype(BF16), g2_ref[...])

    br = lax.broadcasted_iota(jnp.int32, (gw, gw), 0) // HEAD_DIM
    bc = lax.broadcasted_iota(jnp.int32, (gw, gw), 1) // HEAD_DIM
    same_head = br == bc
    head_ones = jnp.where(same_head, 1.0, 0.0).astype(BF16)

    def seg_sum(t):
        return jnp.concatenate(
            [_dot_exact_rhs(t[:, g * gw:(g + 1) * gw], head_ones, terms=2)
             for g in range(n_groups)], axis=1)

    kk = k * kk_ref[...]
    kk = kk / jnp.maximum(jnp.sqrt(seg_sum(kk * kk)), 1e-12)
    k = k * (1.0 + (a - 1.0) * ka_ref[...])
    aa = -kk
    bb = kk * a
    bonus = seg_sum(r * k * rk_ref[...]) * v

    tr = lax.broadcasted_iota(jnp.int32, (ch, ch), 0)
    ts = lax.broadcasted_iota(jnp.int32, (ch, ch), 1)
    tri_incl = jnp.where(tr >= ts, 1.0, 0.0).astype(BF16)
    gr = lax.broadcasted_iota(jnp.int32, (ch, gw), 0)
    gs = lax.broadcasted_iota(jnp.int32, (ch, gw), 1) % ch
    lower = gr >= gs
    strict = gr > gs
    eye = jnp.where(gr == gs, 1.0, 0.0)

    def block_diag(t):
        tiled = jnp.concatenate([t] * (gw // ch), axis=0)
        return jnp.where(same_head, tiled, 0.0).astype(BF16)

    stack = lambda x, y: jnp.concatenate([x, y], axis=0).astype(BF16)
    n_chunks = tc // ch
    chains = [(c, grp) for c in range(n_chunks) for grp in range(n_groups)]
    cs = {}
    for c in range(n_chunks):
        sl = slice(c * ch, (c + 1) * ch)
        ld = log_decay[sl]
        cum = _dot_exact_lhs(tri_incl, ld)
        cum_end = cum[ch - 1:ch, :]
        p_inv = jnp.exp(-cum)
        p_rest = jnp.exp(cum_end - cum)
        cs[c] = dict(p_end=jnp.exp(cum_end), a_t=aa[sl] * jnp.exp(cum - ld),
                     r_t=r[sl] * jnp.exp(cum), b_t=bb[sl] * p_inv, k_t=k[sl] * p_inv,
                     b_h=bb[sl] * p_rest, k_h=k[sl] * p_rest, v=v[sl])
    st = {}
    for c, grp in chains:
        gl = slice(grp * gw, (grp + 1) * gw)
        d = {name: val[:, gl] for name, val in cs[c].items()}
        ar = stack(d["a_t"], d["r_t"])
        sb = _dot_nt(ar, block_diag(d["b_t"]))
        sk = _dot_nt(ar, block_diag(d["k_t"]))
        d["a_ab"] = jnp.where(strict, sb[:ch], 0.0)
        d["a_rb"] = jnp.where(lower, sb[ch:], 0.0)
        a_ak = jnp.where(strict, sk[:ch], 0.0)
        a_rk = jnp.where(lower, sk[ch:], 0.0)
        av = _dot(stack(a_ak, a_rk), block_diag(d["v"]))
        d["av"], d["rv"] = av[:ch], av[ch:]
        d["inv"] = eye + d["a_ab"]
        d["power"] = d["a_ab"]
        st[c, grp] = d
    levels = ch.bit_length() - 1
    for lvl in range(levels):
        for key in chains:
            d = st[key]
            rhs = block_diag(d["power"])
            if lvl == 0:
                d["power"] = _dot(d["power"].astype(BF16), rhs)
            elif lvl < levels - 1:
                both = _dot(stack(d["power"], d["inv"]), rhs)
                d["power"] = both[:ch]
                d["inv"] = d["inv"] + both[ch:]
            else:
                d["inv"] = d["inv"] + _dot(d["inv"].astype(BF16), rhs)
    for key in chains:
        d = st[key]
        inv = d["inv"].astype(BF16)
        d["ta"] = _dot(inv, block_diag(d["a_t"]))
        d["u0"] = _dot(inv, block_diag(d["av"]))
    for key in chains:
        d = st[key]
        a_rb = d["a_rb"].astype(BF16)
        d["query"] = d["r_t"] + _dot(a_rb, block_diag(d["ta"]))
        d["y0"] = d["rv"] + _dot(a_rb, block_diag(d["u0"]))
        d["mix"] = jnp.where(same_head, _dot_tn(d["ta"].astype(BF16), d["b_h"].astype(BF16)), 0.0)
        d["add"] = jnp.where(same_head, _dot_tn(stack(d["u0"], d["v"]),
                                                stack(d["b_h"], d["k_h"])), 0.0)
    for grp in range(n_groups):
        gl = slice(grp * gw, (grp + 1) * gw)
        g_state = state_ref[grp]
        for c in range(n_chunks):
            d = st[c, grp]
            g_bf = g_state.astype(BF16)
            ybuf_ref[c * ch:(c + 1) * ch, gl] = _dot_nt(d["query"].astype(BF16), g_bf) + d["y0"]
            g_state = g_state * d["p_end"] + _dot(g_bf, d["mix"].astype(BF16)) + d["add"]
        state_ref[grp] = g_state

    y = ybuf_ref[...]
    inv_n = 1.0 / HEAD_DIM
    mean = seg_sum(y) * inv_n
    d = y - mean
    var = seg_sum(d * d) * inv_n
    yn = d * lax.rsqrt(var + GN_EPS) * lnw_ref[...] + lnb_ref[...]
    y_ref[...] = ((yn + bonus) * gate).astype(BF16)


def _rwkv_mix(rz, mu, w0, w2p, a0, a2p, g2, k_k, k_a, r_k, ln_w, ln_b, batch):
    n = rz.shape[0]
    t = n // batch
    tc = RWKV_TILE
    nt = t // tc
    vec = lambda w: _const_spec((1, w))
    return pl.pallas_call(
        _rwkv_kernel,
        grid=(batch, nt),
        in_specs=[pl.BlockSpec((tc, RWKV_IN), lambda b, i: (b * nt + i, 0)),
                  vec(RWKV_IN), vec(RWKV_WIDTH), _const_spec((LANES, RWKV_WIDTH)),
                  vec(RWKV_WIDTH), _const_spec((LANES, RWKV_WIDTH)),
                  _const_spec((GATE_LORA, RWKV_WIDTH)),
                  vec(RWKV_WIDTH), vec(RWKV_WIDTH), vec(RWKV_WIDTH), vec(RWKV_WIDTH),
                  vec(RWKV_WIDTH)],
        out_specs=pl.BlockSpec((tc, RWKV_WIDTH), lambda b, i: (b * nt + i, 0)),
        out_shape=jax.ShapeDtypeStruct((n, RWKV_WIDTH), BF16),
        scratch_shapes=[pltpu.VMEM((RWKV_WIDTH // RWKV_GROUP, RWKV_GROUP, RWKV_GROUP), F32),
                        pltpu.VMEM((8, RWKV_IN), F32),
                        pltpu.VMEM((tc, RWKV_WIDTH), F32)],
        compiler_params=_params("arbitrary", "arbitrary"),
        name="rwkv_mix",
    )(rz, mu, w0, w2p, a0, a2p, g2, k_k, k_a, r_k, ln_w, ln_b)


def _rope_table_kernel(pos_ref, freq_ref, cos_ref, sin_ref):
    ang = pos_ref[...] * freq_ref[...]
    s = jnp.sin(ang)
    d = lax.broadcasted_iota(jnp.int32, ang.shape, 1) % HEAD_DIM
    half = ROPE_DIMS // 2
    cos_ref[...] = jnp.where(d < ROPE_DIMS, jnp.cos(ang), 1.0)
    sin_ref[...] = jnp.where(d < half, -s, jnp.where(d < ROPE_DIMS, s, 0.0))


def _rope_tables(positions):
    n = positions.size
    tm = 2048
    pos = jnp.broadcast_to(positions.reshape(n, 1).astype(F32), (n, LANES))
    half = ROPE_DIMS // 2
    inv_freq = ROPE_THETA ** (-jnp.arange(0, ROPE_DIMS, 2, dtype=F32) / ROPE_DIMS)
    d = jnp.arange(LANES) % HEAD_DIM
    freq = jnp.where(d < ROPE_DIMS, inv_freq[d % half], 0.0).reshape(1, LANES)
    row = pl.BlockSpec((tm, LANES), lambda i: (i, 0))
    return pl.pallas_call(
        _rope_table_kernel,
        grid=(n // tm,),
        in_specs=[row, _const_spec((1, LANES))],
        out_specs=[row, row],
        out_shape=[jax.ShapeDtypeStruct((n, LANES), F32)] * 2,
        compiler_params=_params("arbitrary"),
        name="rope_tables",
    )(pos, freq)


def _odd_in_kernel(x_ref, g_ref, w_ref, cos_ref, sin_ref, *refs):
    outs, perm_ref = refs[:-1], refs[-1]
    tm = x_ref.shape[0]
    h = _rms_norm(x_ref[...], g_ref[...]).astype(BF16)
    reps = DIL_WIDTH // LANES
    cos = jnp.concatenate([cos_ref[...]] * reps, axis=1)
    sin = jnp.concatenate([sin_ref[...]] * reps, axis=1)
    half = ROPE_DIMS // 2
    first_half = lax.broadcasted_iota(jnp.int32, (tm, DIL_WIDTH), 1) % HEAD_DIM < half

    def rotary(t):
        partner = jnp.where(first_half, pltpu.roll(t, DIL_WIDTH - half, axis=1),
                            pltpu.roll(t, half, axis=1))
        return t * cos + partner * sin

    def emit(idx, val):
        outs[idx][...] = val.astype(BF16)
        for c in range(reps):
            perm_ref[c] = val[:, c * LANES:(c + 1) * LANES]
        for gi, (_, d) in enumerate(DILATED_GROUPS[1:], start=1):
            ref = outs[gi * 3 + idx]
            for rho in range(d):
                for c in range(reps):
                    ref[0, rho, :, c * LANES:(c + 1) * LANES] = perm_ref[
                        c, pl.ds(rho, tm // d, stride=d), :].astype(BF16)

    emit(0, rotary(_dot(h, w_ref[:, 0:DIL_WIDTH])) * (LOG2_E * HEAD_DIM ** -0.5))
    emit(1, rotary(_dot(h, w_ref[:, DIL_WIDTH:2 * DIL_WIDTH])))
    emit(2, _dot(h, w_ref[:, 2 * DIL_WIDTH:]))


def _residue_spec(d, tm, width, tiles_per_batch):
    return pl.BlockSpec((1, d, tm // d, width),
                        lambda i: (i // tiles_per_batch, 0, i % tiles_per_batch, 0))


def _odd_in(x2, g, w, cos, sin, batch):
    n = x2.shape[0]
    t = n // batch
    tm = ROW_TILE
    row = lambda w_: pl.BlockSpec((tm, w_), lambda i: (i, 0))
    out_specs = [row(DIL_WIDTH)] * 3
    out_shape = [jax.ShapeDtypeStruct((n, DIL_WIDTH), BF16)] * 3
    for _, d in DILATED_GROUPS[1:]:
        out_specs += [_residue_spec(d, tm, DIL_WIDTH, t // tm)] * 3
        out_shape += [jax.ShapeDtypeStruct((batch, d, t // d, DIL_WIDTH), BF16)] * 3
    return pl.pallas_call(
        _odd_in_kernel,
        grid=(n // tm,),
        in_specs=[row(D_MODEL), _const_spec((1, D_MODEL)), _const_spec((D_MODEL, 3 * DIL_WIDTH)),
                  row(LANES), row(LANES)],
        out_specs=out_specs,
        out_shape=out_shape,
        scratch_shapes=[pltpu.VMEM((DIL_WIDTH // LANES, tm, LANES), F32)],
        compiler_params=_params("arbitrary"),
        name="odd_in",
    )(x2, g, w, cos, sin)


def _dilated_kernel(q_ref, kc_ref, kp_ref, vc_ref, vp_ref, o_ref, lse_ref):
    blk = pl.program_id(2)
    sp = DIL_SPAN
    qrows = q_ref.shape[2]
    lane = lax.broadcasted_iota(jnp.int32, (sp, LANES), 1)
    head0 = lane < HEAD_DIM
    qi = lax.broadcasted_iota(jnp.int32, (2 * sp, 2 * sp), 0) % sp
    ki = lax.broadcasted_iota(jnp.int32, (2 * sp, 2 * sp), 1)
    band = (ki >= qi) & (ki <= qi + sp)
    bias = jnp.where(band, 0.0, NEG_INF)
    bias_first = jnp.where(band & (ki >= jnp.where(blk > 0, 0, sp)), 0.0, NEG_INF)
    n_steps = DIL_WIDTH // LANES // DIL_PAIRS_PER_STEP

    for sub in range(qrows // sp):
        r0 = sub * sp

        def step(g, lse_acc, r0=r0, sub=sub):
            for pp in range(DIL_PAIRS_PER_STEP):
                p = g * DIL_PAIRS_PER_STEP + pp
                cols = pl.ds(pl.multiple_of(p * LANES, LANES), LANES)
                q = q_ref[0, 0, r0:r0 + sp, cols]
                if sub == 0:
                    kb = jnp.concatenate([kp_ref[0, 0, qrows - sp:qrows, cols],
                                          kc_ref[0, 0, 0:sp, cols]], axis=0)
                    vb = jnp.concatenate([vp_ref[0, 0, qrows - sp:qrows, cols],
                                          vc_ref[0, 0, 0:sp, cols]], axis=0)
                else:
                    kb = kc_ref[0, 0, r0 - sp:r0 + sp, cols]
                    vb = vc_ref[0, 0, r0 - sp:r0 + sp, cols]
                zero = jnp.zeros_like(q)
                q2 = jnp.concatenate([jnp.where(head0, q, zero), jnp.where(head0, zero, q)], axis=0)
                s = _dot_nt(q2, kb) + (bias_first if sub == 0 else bias)
                m = jnp.max(s, axis=1, keepdims=True)
                e = jnp.exp2(s - m)
                den = jnp.sum(e, axis=1, keepdims=True)
                o2 = _dot(e.astype(BF16), vb) * (1.0 / den)
                lse = m + jnp.log2(den)
                o_ref[0, 0, r0:r0 + sp, cols] = jnp.where(head0, o2[:sp], o2[sp:]).astype(o_ref.dtype)
                lse_acc = jnp.where(lane == 2 * p, lse[:sp],
                                    jnp.where(lane == 2 * p + 1, lse[sp:], lse_acc))
            return lse_acc

        lse_all = lax.fori_loop(0, n_steps, step, jnp.zeros((sp, LANES), F32))
        lse_ref[0, 0, r0:r0 + sp, :] = lse_all


def _dilated_branch(q, k, v, batch):
    _, dilation, length, _ = q.shape
    qrows = min(DIL_QROWS, length)
    nb = length // qrows
    cur = pl.BlockSpec((1, 1, qrows, DIL_WIDTH), lambda b, r, i: (b, r, i, 0))
    prev = pl.BlockSpec((1, 1, qrows, DIL_WIDTH), lambda b, r, i: (b, r, jnp.maximum(i - 1, 0), 0))
    return pl.pallas_call(
        _dilated_kernel,
        grid=(batch, dilation, nb),
        in_specs=[cur, cur, prev, cur, prev],
        out_specs=[cur, pl.BlockSpec((1, 1, qrows, LANES), lambda b, r, i: (b, r, i, 0))],
        out_shape=[jax.ShapeDtypeStruct((batch, dilation, length, DIL_WIDTH), BF16),
                   jax.ShapeDtypeStruct((batch, dilation, length, LANES), F32)],
        compiler_params=_params("arbitrary", "arbitrary", "arbitrary"),
        name=f"dilated_d{dilation}",
    )(q, k, k, v, v)


def _gelu_tanh(x):
    return 0.5 * x * (1.0 + jnp.tanh(0.7978845608028654 * (x + 0.044715 * x * x * x)))


def _finish_layer(m, x_ref, gmp_ref, gfp_ref, gfo_ref, wup_ref, cw_ref, cb_ref, wdn_ref, out_ref,
                  tail_ref, act_ref, tiles_per_batch):
    i = pl.program_id(0)
    tm = x_ref.shape[0]
    x1 = x_ref[...] + _rms_norm(m, gmp_ref[...])
    h = _rms_norm(x1, gfp_ref[...]).astype(BF16)

    @pl.when(i % tiles_per_batch == 0)
    def _():
        tail_ref[...] = jnp.zeros_like(tail_ref)

    ck = FFN_CHUNK
    rows = lax.broadcasted_iota(jnp.int32, (tm, ck), 0)

    def conv(col):
        u = _dot(h, wup_ref[:, col:col + ck])
        t1 = tail_ref[7:8, col:col + ck]
        t2 = tail_ref[6:7, col:col + ck]
        u1 = jnp.where(rows == 0, t1, pltpu.roll(u, 1, axis=0))
        u2 = jnp.where(rows == 0, t2, jnp.where(rows == 1, t1, pltpu.roll(u, 2, axis=0)))
        tail_ref[:, col:col + ck] = u[tm - 8:tm, :]
        return (cb_ref[:, col:col + ck] + cw_ref[2:3, col:col + ck] * u
                + cw_ref[1:2, col:col + ck] * u1 + cw_ref[0:1, col:col + ck] * u2)

    for c in range(D_FF // ck):
        gate = conv(c * ck)
        val = conv(D_FF + c * ck)
        act_ref[:, c * ck:(c + 1) * ck] = (_gelu_tanh(gate) * val).astype(BF16)
    f = _dot(act_ref[...], wdn_ref[...])
    out_ref[...] = x1 + _rms_norm(f, gfo_ref[...])


def _even_post_kernel(yft_ref, yr_ref, x_ref, wo_ref, *rest, tiles_per_batch):
    m = _dot_tn(yft_ref[...], wo_ref[0:FOX_WIDTH, :]) + _dot(yr_ref[...], wo_ref[FOX_WIDTH:, :])
    _finish_layer(m, x_ref, *rest, tiles_per_batch=tiles_per_batch)


def _time_order(src_ref, perm_ref):
    _, d, rows, width = src_ref.shape
    if d == 1:
        return src_ref[0, 0].astype(F32)
    blocks = width // LANES
    for rho in range(d):
        for c in range(blocks):
            perm_ref[c, pl.ds(rho, rows, stride=d), :] = src_ref[
                0, rho, :, c * LANES:(c + 1) * LANES].astype(F32)
    return jnp.concatenate([perm_ref[c] for c in range(blocks)], axis=1)


def _odd_post_kernel(*refs, tiles_per_batch):
    ng = len(DILATED_GROUPS)
    o_refs, l_refs = refs[:ng], refs[ng:2 * ng]
    x_ref, wo_ref = refs[2 * ng:2 * ng + 2]
    rest, (perm_o_ref, perm_l_ref) = refs[2 * ng + 2:-2], refs[-2:]
    lses = [_time_order(l_ref, perm_l_ref) for l_ref in l_refs]
    m = functools.reduce(jnp.maximum, lses)
    es = [jnp.exp2(l - m) for l in lses]
    inv = 1.0 / functools.reduce(jnp.add, es)
    hr = lax.broadcasted_iota(jnp.int32, (LANES, DIL_WIDTH), 0)
    hc = lax.broadcasted_iota(jnp.int32, (LANES, DIL_WIDTH), 1) // HEAD_DIM
    expand = jnp.where(hr == hc, 1.0, 0.0).astype(BF16)
    o = None
    for e, o_ref in zip(es, o_refs):
        term = _dot_exact_rhs(e * inv, expand, terms=2) * _time_order(o_ref, perm_o_ref)
        o = term if o is None else o + term
    _finish_layer(_dot(o.astype(BF16), wo_ref[...]), x_ref, *rest, tiles_per_batch=tiles_per_batch)


def _post_ffn(mixer_outs, x2, wo, g_mix_post, g_ffn_pre, g_ffn_post, w_up, conv_w, conv_b, w_dn,
              batch, even):
    n = x2.shape[0]
    tm = ROW_TILE
    tpb = n // batch // tm
    row = lambda w: pl.BlockSpec((tm, w), lambda i: (i, 0))
    vec = _const_spec((1, D_MODEL))
    def spec(a):
        if a.ndim == 4:
            return _residue_spec(a.shape[1], tm, a.shape[3], tpb)
        if a.shape[0] == n:
            return row(a.shape[1])
        return pl.BlockSpec((a.shape[0], tm), lambda i: (0, i))

    mixer_specs = [spec(a) for a in mixer_outs]
    body = _even_post_kernel if even else _odd_post_kernel
    scratch = [pltpu.VMEM((8, 2 * D_FF), F32), pltpu.VMEM((tm, D_FF), BF16)]
    if not even:
        scratch += [pltpu.VMEM((DIL_WIDTH // LANES, tm, LANES), F32), pltpu.VMEM((1, tm, LANES), F32)]
    return pl.pallas_call(
        functools.partial(body, tiles_per_batch=tpb),
        grid=(n // tm,),
        in_specs=mixer_specs + [row(D_MODEL), _const_spec((D_MODEL, D_MODEL)), vec, vec, vec,
                                _const_spec((D_MODEL, 2 * D_FF)), _const_spec((8, 2 * D_FF)),
                                _const_spec((1, 2 * D_FF)), _const_spec((D_FF, D_MODEL))],
        out_specs=row(D_MODEL),
        out_shape=jax.ShapeDtypeStruct((n, D_MODEL), F32),
        scratch_shapes=scratch,
        compiler_params=_params("arbitrary"),
        name="post_ffn_even" if even else "post_ffn_odd",
    )(*mixer_outs, x2, wo, g_mix_post, g_ffn_pre, g_ffn_post, w_up, conv_w, conv_b, w_dn)


def _pad_rows(a, rows, before=0):
    return jnp.pad(a, ((before, rows - a.shape[0] - before), (0, 0)))


def kernel(x, positions, norm_mix_pre, norm_mix_post, norm_ffn_pre, norm_ffn_post, even_w_in, fox_forget_bias, rwkv_mu, rwkv_w0, rwkv_w2, rwkv_a0, rwkv_a2, rwkv_g2, rwkv_k_k, rwkv_k_a, rwkv_r_k, rwkv_ln_w, rwkv_ln_b, even_w_out, odd_w_in, odd_w_out, ffn_w_up, ffn_conv_w, ffn_conv_b, ffn_w_down):
    batch, seq, _ = x.shape
    n = batch * seq
    depth = norm_mix_pre.shape[0]
    x2 = x.reshape(n, D_MODEL)
    rope = None
    vec = lambda a: a.reshape(1, -1)
    for layer in range(depth):
        i = layer // 2
        g_pre = vec(norm_mix_pre[layer])
        if layer % 2 == 0:
            w_in = even_w_in[i]
            f0 = 3 * FOX_WIDTH
            w_pack = jnp.concatenate(
                [w_in[:, FOX_WIDTH:2 * FOX_WIDTH],
                 jnp.pad(w_in[:, f0:FOX_IN], ((0, 0), (0, LANES - N_FOX_HEADS))),
                 w_in[:, FOX_IN:]], axis=1).astype(BF16)
            w_qv_t = jnp.concatenate([w_in[:, :FOX_WIDTH], w_in[:, 2 * FOX_WIDTH:f0]],
                                     axis=1).T.astype(BF16)
            fb = jnp.pad(fox_forget_bias[i], (0, LANES - N_FOX_HEADS)).reshape(1, LANES)
            qt, kx, vt, ct, rz = _even_in(x2, g_pre, w_pack, w_qv_t, fb, batch)
            y_fox = _fox_attention(qt, kx, vt, ct, batch)
            w2p = _pad_rows(rwkv_w2[i], LANES).astype(BF16)
            a2p = _pad_rows(rwkv_a2[i], LANES, before=DECAY_LORA).astype(BF16)
            y_rwkv = _rwkv_mix(rz, vec(rwkv_mu[i]), vec(rwkv_w0[i]), w2p, vec(rwkv_a0[i]), a2p,
                               rwkv_g2[i].astype(BF16), vec(rwkv_k_k[i]), vec(rwkv_k_a[i]),
                               vec(rwkv_r_k[i]), vec(rwkv_ln_w[i]), vec(rwkv_ln_b[i]), batch)
            mixer_outs = [y_fox, y_rwkv]
            w_out = even_w_out[i]
        else:
            if rope is None:
                rope = _rope_tables(positions)
            qkv = _odd_in(x2, g_pre, odd_w_in[i].astype(BF16), *rope, batch)
            outs, lses = [], []
            for gi, (_, d) in enumerate(DILATED_GROUPS):
                q, k, v = [a.reshape(batch, d, -1, DIL_WIDTH) for a in qkv[3 * gi:3 * gi + 3]]
                o, lse = _dilated_branch(q, k, v, batch)
                outs.append(o)
                lses.append(lse)
            mixer_outs = outs + lses
            w_out = odd_w_out[i]
        x2 = _post_ffn(mixer_outs, x2, w_out.astype(BF16), vec(norm_mix_post[layer]),
                       vec(norm_ffn_pre[layer]), vec(norm_ffn_post[layer]),
                       ffn_w_up[layer].astype(BF16), _pad_rows(ffn_conv_w[layer], 8),
                       vec(ffn_conv_b[layer]), ffn_w_down[layer].astype(BF16), batch,
                       even=layer % 2 == 0)
    return x2.reshape(batch, seq, D_MODEL)
```

```python
import functools

import jax
import jax.numpy as jnp
from jax import lax
from jax.experimental import pallas as pl
from jax.experimental.pallas import tpu as pltpu

F32 = jnp.float32
BF16 = jnp.bfloat16

D_MODEL = 1024
HEAD_DIM = 64
N_FOX_HEADS = 8
FOX_WIDTH = 512
RWKV_WIDTH = 512
DECAY_LORA = 64
ICLR_LORA = 64
GATE_LORA = 128
RWKV_IN = 3 * RWKV_WIDTH + DECAY_LORA + ICLR_LORA + GATE_LORA
FOX_IN = 3 * FOX_WIDTH + N_FOX_HEADS
N_DIL_HEADS = 16
DIL_WIDTH = 1024
DILATED_GROUPS = ((128, 1), (512, 4), (2048, 16))
DIL_SPAN = 128
ROPE_THETA = 500000.0
ROPE_DIMS = 16
D_FF = 2816
RMS_EPS = 1e-6
GN_EPS = 64e-5
NEG_INF = -1e30
LOG2_E = 1.4426950408889634

LANES = 128
V7X_VMEM_LIMIT = 56 * 1024 * 1024

ROW_TILE = 512
FOX_TILE = 1024
RWKV_TILE = 256
RWKV_CHUNK = 64
RWKV_GROUP = 256
DIL_QROWS = 512
DIL_PAIRS_PER_STEP = 8
FFN_CHUNK = 256


def _dot(a, b):
    return jnp.dot(a, b, preferred_element_type=F32)


def _dot_nt(a, b):
    return lax.dot_general(a, b, (((1,), (1,)), ((), ())), preferred_element_type=F32)


def _dot_tn(a, b):
    return lax.dot_general(a, b, (((0,), (0,)), ((), ())), preferred_element_type=F32)


def _split(x, terms):
    parts = []
    for _ in range(terms):
        part = x.astype(BF16)
        parts.append(part)
        x = x - part.astype(F32)
    return parts


def _dot_exact_lhs(sel, x, terms=3):
    return functools.reduce(jnp.add, [_dot(sel, part) for part in _split(x, terms)])


def _dot_exact_rhs(x, sel, terms=3):
    return functools.reduce(jnp.add, [_dot(part, sel) for part in _split(x, terms)])


def _rms_norm(x, g):
    return x * lax.rsqrt(jnp.mean(x * x, axis=-1, keepdims=True) + RMS_EPS) * g


def _softplus(x):
    return jnp.maximum(x, 0.0) + jnp.log1p(jnp.exp(-jnp.abs(x)))


def _sigmoid(x):
    return 1.0 / (1.0 + jnp.exp(-x))


def _const_spec(shape):
    nd = len(shape)
    return pl.BlockSpec(shape, lambda *_: (0,) * nd, pipeline_mode=pl.Buffered(1))


def _params(*sem):
    return pltpu.CompilerParams(dimension_semantics=sem, vmem_limit_bytes=V7X_VMEM_LIMIT)


EVEN_PACKED = FOX_WIDTH + LANES + RWKV_IN

FOX_PAIRS = FOX_WIDTH // LANES
FOX_KEY_LANES = 2 * LANES
FOX_BIAS_TERMS = 3


def _fox_key_bias_selectors():
    r = lax.broadcasted_iota(jnp.int32, (LANES, FOX_PAIRS * LANES), 0)
    cidx = lax.broadcasted_iota(jnp.int32, (LANES, FOX_PAIRS * LANES), 1)
    pair, slot = cidx // LANES, cidx % LANES
    return [jnp.where((r // 2 == pair) & (r < N_FOX_HEADS)
                      & (slot == FOX_BIAS_TERMS * (r % 2) + t), 1.0, 0.0).astype(BF16)
            for t in range(FOX_BIAS_TERMS)]


def _even_in_kernel(x_ref, g_ref, w_ref, wt_ref, fb_ref, qt_ref, kx_ref, vt_ref, ct_ref, rz_ref,
                    carry_ref, *, tiles_per_batch):
    i = pl.program_id(0)
    tm = x_ref.shape[0]
    h = _rms_norm(x_ref[...], g_ref[...]).astype(BF16)
    qt_ref[...] = (_dot_nt(wt_ref[0:FOX_WIDTH, :], h) * (LOG2_E * HEAD_DIM ** -0.5)).astype(BF16)
    vt_ref[...] = _dot_nt(wt_ref[FOX_WIDTH:, :], h).astype(BF16)
    f = _dot(h, w_ref[:, FOX_WIDTH:FOX_WIDTH + LANES]) + fb_ref[...]
    log2_f = -_softplus(-f) * LOG2_E

    @pl.when(i % tiles_per_batch == 0)
    def _():
        carry_ref[...] = jnp.zeros_like(carry_ref)

    row = lax.broadcasted_iota(jnp.int32, (tm, tm), 0)
    col = lax.broadcasted_iota(jnp.int32, (tm, tm), 1)
    tri = jnp.where(row >= col, 1.0, 0.0).astype(BF16)
    c = _dot_exact_lhs(tri, log2_f) + carry_ref[0:1, :]
    carry_ref[...] = jnp.broadcast_to(c[tm - 1:tm, :], carry_ref.shape)
    ct_ref[...] = jnp.transpose(c)[0:N_FOX_HEADS, :]

    k = _dot(h, w_ref[:, 0:FOX_WIDTH])
    bias = functools.reduce(jnp.add, [_dot(part, sel) for part, sel in
                                      zip(_split(-c, FOX_BIAS_TERMS), _fox_key_bias_selectors())])
    slot = lax.broadcasted_iota(jnp.int32, (tm, FOX_PAIRS * LANES), 1) % LANES
    ones = (slot >= 2 * FOX_BIAS_TERMS) & (slot < 3 * FOX_BIAS_TERMS)
    bias = jnp.where(ones, 1.0, bias).astype(BF16)
    for p in range(FOX_PAIRS):
        kx_ref[:, p * FOX_KEY_LANES:p * FOX_KEY_LANES + LANES] = (
            k[:, p * LANES:(p + 1) * LANES].astype(BF16))
        kx_ref[:, p * FOX_KEY_LANES + LANES:(p + 1) * FOX_KEY_LANES] = (
            bias[:, p * LANES:(p + 1) * LANES])
    rz_ref[...] = _dot(h, w_ref[:, FOX_WIDTH + LANES:])


def _even_in(x2, g, w_pack, w_qv_t, fb, batch):
    n = x2.shape[0]
    tm = ROW_TILE
    row = lambda w: pl.BlockSpec((tm, w), lambda i: (i, 0))
    col = lambda h: pl.BlockSpec((h, tm), lambda i: (0, i))
    return pl.pallas_call(
        functools.partial(_even_in_kernel, tiles_per_batch=n // batch // tm),
        grid=(n // tm,),
        in_specs=[row(D_MODEL), _const_spec((1, D_MODEL)), _const_spec((D_MODEL, EVEN_PACKED)),
                  _const_spec((2 * FOX_WIDTH, D_MODEL)), _const_spec((1, LANES))],
        out_specs=[col(FOX_WIDTH), row(FOX_PAIRS * FOX_KEY_LANES), col(FOX_WIDTH),
                   col(N_FOX_HEADS), row(RWKV_IN)],
        out_shape=[jax.ShapeDtypeStruct((FOX_WIDTH, n), BF16),
                   jax.ShapeDtypeStruct((n, FOX_PAIRS * FOX_KEY_LANES), BF16),
                   jax.ShapeDtypeStruct((FOX_WIDTH, n), BF16),
                   jax.ShapeDtypeStruct((N_FOX_HEADS, n), F32),
                   jax.ShapeDtypeStruct((n, RWKV_IN), F32)],
        scratch_shapes=[pltpu.VMEM((8, LANES), F32)],
        compiler_params=_params("arbitrary"),
        name="even_in",
    )(x2, g, w_pack, w_qv_t, fb)


def _fox_kernel(qt_ref, kx_ref, vt_ref, ct_ref, o_ref):
    hp = pl.program_id(1)
    qi = pl.program_id(2)
    tq = qt_ref.shape[1]
    qt = qt_ref[...]
    row = lax.broadcasted_iota(jnp.int32, (LANES, tq), 0)
    head0 = row < HEAD_DIM
    zero = jnp.zeros_like(qt)

    def bias_rows(hh):
        c = ct_ref[pl.ds(2 * hp + hh, 1), :]
        terms = [part.astype(F32) for part in _split(c, FOX_BIAS_TERMS)]
        b = jnp.zeros((LANES, tq), F32)
        for t, term in enumerate(terms):
            b = jnp.where(row == 2 * FOX_BIAS_TERMS + t, term, b)
        own = (row >= FOX_BIAS_TERMS * hh) & (row < FOX_BIAS_TERMS * (hh + 1))
        return jnp.where(own, 1.0, b).astype(BF16)

    qxt = jnp.concatenate([
        jnp.concatenate([jnp.where(head0, qt, zero), bias_rows(0)], axis=0),
        jnp.concatenate([jnp.where(head0, zero, qt), bias_rows(1)], axis=0)], axis=1)
    tk = tq
    key_idx = lax.broadcasted_iota(jnp.int32, (tk, 2 * tq), 0)
    query_idx = lax.broadcasted_iota(jnp.int32, (tk, 2 * tq), 1) % tq
    causal = key_idx <= query_idx

    def step(j, carry, masked):
        m, l, acc = carry
        k0 = pl.multiple_of(j * tk, tk)
        s = _dot(kx_ref[pl.ds(k0, tk), :], qxt)
        if masked:
            s = jnp.where(causal, s, NEG_INF)
        m_new = jnp.maximum(m, jnp.max(s, axis=0, keepdims=True))
        p = jnp.exp2(s - m_new)
        alpha = jnp.exp2(m - m_new)
        l = alpha * l + jnp.sum(p, axis=0, keepdims=True)
        acc = alpha * acc + _dot(vt_ref[:, pl.ds(k0, tk)], p.astype(BF16))
        return m_new, l, acc

    init = (jnp.full((1, 2 * tq), NEG_INF, F32), jnp.zeros((1, 2 * tq), F32),
            jnp.zeros((LANES, 2 * tq), F32))
    carry = lax.fori_loop(0, qi, lambda j, c: step(j, c, False), init)
    _, l, acc = step(qi, carry, True)
    out = acc / l
    o_ref[...] = jnp.where(head0, out[:, :tq], out[:, tq:]).astype(BF16)


def _fox_attention(qt, kx, vt, ct, batch):
    n = qt.shape[1]
    t = n // batch
    tq = FOX_TILE
    nq = t // tq
    return pl.pallas_call(
        _fox_kernel,
        grid=(batch, FOX_PAIRS, nq),
        in_specs=[
            pl.BlockSpec((LANES, tq), lambda b, p, i: (p, b * nq + i)),
            pl.BlockSpec((t, FOX_KEY_LANES), lambda b, p, i: (b, p)),
            pl.BlockSpec((LANES, t), lambda b, p, i: (p, b)),
            pl.BlockSpec((N_FOX_HEADS, tq), lambda b, p, i: (0, b * nq + i)),
        ],
        out_specs=pl.BlockSpec((LANES, tq), lambda b, p, i: (p, b * nq + i)),
        out_shape=jax.ShapeDtypeStruct((FOX_WIDTH, n), BF16),
        compiler_params=_params("arbitrary", "arbitrary", "arbitrary"),
        name="fox_attention",
    )(qt, kx, vt, ct)


def _rwkv_kernel(rz_ref, mu_ref, w0_ref, w2_ref, a0_ref, a2_ref, g2_ref, kk_ref, ka_ref, rk_ref,
                 lnw_ref, lnb_ref, y_ref, state_ref, tail_ref, ybuf_ref):
    tc = rz_ref.shape[0]
    ch = RWKV_CHUNK
    gw = RWKV_GROUP
    n_groups = RWKV_WIDTH // gw

    @pl.when(pl.program_id(1) == 0)
    def _():
        state_ref[...] = jnp.zeros_like(state_ref)
        tail_ref[...] = jnp.zeros_like(tail_ref)

    z = rz_ref[...]
    rows = lax.broadcasted_iota(jnp.int32, z.shape, 0)
    z_prev = jnp.where(rows == 0, tail_ref[7:8, :], pltpu.roll(z, 1, axis=0))
    tail_ref[...] = z[tc - 8:tc, :]
    z = z + (z_prev - z) * mu_ref[...]
    r = z[:, 0:RWKV_WIDTH]
    k = z[:, RWKV_WIDTH:2 * RWKV_WIDTH]
    v = z[:, 2 * RWKV_WIDTH:3 * RWKV_WIDTH]
    lo = z[:, 3 * RWKV_WIDTH:3 * RWKV_WIDTH + LANES]
    g_lo = z[:, 3 * RWKV_WIDTH + LANES:]

    w = w0_ref[...] + _dot(jnp.tanh(lo).astype(BF16), w2_ref[...])
    w = -_softplus(-w) - 0.5
    log_decay = -jnp.exp(w)
    a = _sigmoid(a0_ref[...] + _dot(lo.astype(BF16), a2_ref[...]))
    gate = _dot(_sigmoid(g_lo).astype(BF16), g2_ref[...])

    br = lax.broadcasted_iota(jnp.int32, (gw, gw), 0) // HEAD_DIM
    bc = lax.broadcasted_iota(jnp.int32, (gw, gw), 1) // HEAD_DIM
    same_head = br == bc
    head_ones = jnp.where(same_head, 1.0, 0.0).astype(BF16)

    def seg_sum(t):
        return jnp.concatenate(
            [_dot_exact_rhs(t[:, g * gw:(g + 1) * gw], head_ones, terms=2)
             for g in range(n_groups)], axis=1)

    kk = k * kk_ref[...]
    kk = kk / jnp.maximum(jnp.sqrt(seg_sum(kk * kk)), 1e-12)
    k = k * (1.0 + (a - 1.0) * ka_ref[...])
    aa = -kk
    bb = kk * a
    bonus = seg_sum(r * k * rk_ref[...]) * v

    tr = lax.broadcasted_iota(jnp.int32, (ch, ch), 0)
    ts = lax.broadcasted_iota(jnp.int32, (ch, ch), 1)
    tri_incl = jnp.where(tr >= ts, 1.0, 0.0).astype(BF16)
    gr = lax.broadcasted_iota(jnp.int32, (ch, gw), 0)
    gs = lax.broadcasted_iota(jnp.int32, (ch, gw), 1) % ch
    lower = gr >= gs
    strict = gr > gs
    eye = jnp.where(gr == gs, 1.0, 0.0)

    def block_diag(t):
        tiled = jnp.concatenate([t] * (gw // ch), axis=0)
        return jnp.where(same_head, tiled, 0.0).astype(BF16)

    stack = lambda x, y: jnp.concatenate([x, y], axis=0).astype(BF16)
    n_chunks = tc // ch
    chains = [(c, grp) for c in range(n_chunks) for grp in range(n_groups)]
    cs = {}
    for c in range(n_chunks):
        sl = slice(c * ch, (c + 1) * ch)
        ld = log_decay[sl]
        cum = _dot_exact_lhs(tri_incl, ld)
        cum_end = cum[ch - 1:ch, :]
        p_inv = jnp.exp(-cum)
        p_rest = jnp.exp(cum_end - cum)
        cs[c] = dict(p_end=jnp.exp(cum_end), a_t=aa[sl] * jnp.exp(cum - ld),
                     r_t=r[sl] * jnp.exp(cum), b_t=bb[sl] * p_inv, k_t=k[sl] * p_inv,
                     b_h=bb[sl] * p_rest, k_h=k[sl] * p_rest, v=v[sl])
    st = {}
    for c, grp in chains:
        gl = slice(grp * gw, (grp + 1) * gw)
        d = {name: val[:, gl] for name, val in cs[c].items()}
        ar = stack(d["a_t"], d["r_t"])
        sb = _dot_nt(ar, block_diag(d["b_t"]))
        sk = _dot_nt(ar, block_diag(d["k_t"]))
        d["a_ab"] = jnp.where(strict, sb[:ch], 0.0)
        d["a_rb"] = jnp.where(lower, sb[ch:], 0.0)
        a_ak = jnp.where(strict, sk[:ch], 0.0)
        a_rk = jnp.where(lower, sk[ch:], 0.0)
        av = _dot(stack(a_ak, a_rk), block_diag(d["v"]))
        d["av"], d["rv"] = av[:ch], av[ch:]
        d["inv"] = eye + d["a_ab"]
        d["power"] = d["a_ab"]
        st[c, grp] = d
    levels = ch.bit_length() - 1
    for lvl in range(levels):
        for key in chains:
            d = st[key]
            rhs = block_diag(d["power"])
            if lvl == 0:
                d["power"] = _dot(d["power"].astype(BF16), rhs)
            elif lvl < levels - 1:
                both = _dot(stack(d["power"], d["inv"]), rhs)
                d["power"] = both[:ch]
                d["inv"] = d["inv"] + both[ch:]
            else:
                d["inv"] = d["inv"] + _dot(d["inv"].astype(BF16), rhs)
    for key in chains:
        d = st[key]
        inv = d["inv"].astype(BF16)
        d["ta"] = _dot(inv, block_diag(d["a_t"]))
        d["u0"] = _dot(inv, block_diag(d["av"]))
    for key in chains:
        d = st[key]
        a_rb = d["a_rb"].astype(BF16)
        d["query"] = d["r_t"] + _dot(a_rb, block_diag(d["ta"]))
        d["y0"] = d["rv"] + _dot(a_rb, block_diag(d["u0"]))
        d["mix"] = jnp.where(same_head, _dot_tn(d["ta"].astype(BF16), d["b_h"].astype(BF16)), 0.0)
        d["add"] = jnp.where(same_head, _dot_tn(stack(d["u0"], d["v"]),
                                                stack(d["b_h"], d["k_h"])), 0.0)
    for grp in range(n_groups):
        gl = slice(grp * gw, (grp + 1) * gw)
        g_state = state_ref[grp]
        for c in range(n_chunks):
            d = st[c, grp]
            g_bf = g_state.astype(BF16)
            ybuf_ref[c * ch:(c + 1) * ch, gl] = _dot_nt(d["query"].astype(BF16), g_bf) + d["y0"]
            g_state = g_state * d["p_end"] + _dot(g_bf, d["mix"].astype(BF16)) + d["add"]
        state_ref[grp] = g_state

    y = ybuf_ref[...]
    inv_n = 1.0 / HEAD_DIM
    mean = seg_sum(y) * inv_n
    d = y - mean
    var = seg_sum(d * d) * inv_n
    yn = d * lax.rsqrt(var + GN_EPS) * lnw_ref[...] + lnb_ref[...]
    y_ref[...] = ((yn + bonus) * gate).astype(BF16)


def _rwkv_mix(rz, mu, w0, w2p, a0, a2p, g2, k_k, k_a, r_k, ln_w, ln_b, batch):
    n = rz.shape[0]
    t = n // batch
    tc = RWKV_TILE
    nt = t // tc
    vec = lambda w: _const_spec((1, w))
    return pl.pallas_call(
        _rwkv_kernel,
        grid=(batch, nt),
        in_specs=[pl.BlockSpec((tc, RWKV_IN), lambda b, i: (b * nt + i, 0)),
                  vec(RWKV_IN), vec(RWKV_WIDTH), _const_spec((LANES, RWKV_WIDTH)),
                  vec(RWKV_WIDTH), _const_spec((LANES, RWKV_WIDTH)),
                  _const_spec((GATE_LORA, RWKV_WIDTH)),
                  vec(RWKV_WIDTH), vec(RWKV_WIDTH), vec(RWKV_WIDTH), vec(RWKV_WIDTH),
                  vec(RWKV_WIDTH)],
        out_specs=pl.BlockSpec((tc, RWKV_WIDTH), lambda b, i: (b * nt + i, 0)),
        out_shape=jax.ShapeDtypeStruct((n, RWKV_WIDTH), BF16),
        scratch_shapes=[pltpu.VMEM((RWKV_WIDTH // RWKV_GROUP, RWKV_GROUP, RWKV_GROUP), F32),
                        pltpu.VMEM((8, RWKV_IN), F32),
                        pltpu.VMEM((tc, RWKV_WIDTH), F32)],
        compiler_params=_params("arbitrary", "arbitrary"),
        name="rwkv_mix",
    )(rz, mu, w0, w2p, a0, a2p, g2, k_k, k_a, r_k, ln_w, ln_b)


def _rope_table_kernel(pos_ref, freq_ref, cos_ref, sin_ref):
    ang = pos_ref[...] * freq_ref[...]
    s = jnp.sin(ang)
    d = lax.broadcasted_iota(jnp.int32, ang.shape, 1) % HEAD_DIM
    half = ROPE_DIMS // 2
    cos_ref[...] = jnp.where(d < ROPE_DIMS, jnp.cos(ang), 1.0)
    sin_ref[...] = jnp.where(d < half, -s, jnp.where(d < ROPE_DIMS, s, 0.0))


def _rope_tables(positions):
    n = positions.size
    tm = 2048
    pos = jnp.broadcast_to(positions.reshape(n, 1).astype(F32), (n, LANES))
    half = ROPE_DIMS // 2
    inv_freq = ROPE_THETA ** (-jnp.arange(0, ROPE_DIMS, 2, dtype=F32) / ROPE_DIMS)
    d = jnp.arange(LANES) % HEAD_DIM
    freq = jnp.where(d < ROPE_DIMS, inv_freq[d % half], 0.0).reshape(1, LANES)
    row = pl.BlockSpec((tm, LANES), lambda i: (i, 0))
    return pl.pallas_call(
        _rope_table_kernel,
        grid=(n // tm,),
        in_specs=[row, _const_spec((1, LANES))],
        out_specs=[row, row],
        out_shape=[jax.ShapeDtypeStruct((n, LANES), F32)] * 2,
        compiler_params=_params("arbitrary"),
        name="rope_tables",
    )(pos, freq)


def _odd_in_kernel(x_ref, g_ref, w_ref, cos_ref, sin_ref, *refs):
    n_perm = len(DILATED_GROUPS) - 1
    outs, perm_refs = refs[:-n_perm], refs[-n_perm:]
    tm = x_ref.shape[0]
    h = _rms_norm(x_ref[...], g_ref[...]).astype(BF16)
    cw = 2 * LANES
    cos = jnp.concatenate([cos_ref[...]] * (cw // LANES), axis=1)
    sin = jnp.concatenate([sin_ref[...]] * (cw // LANES), axis=1)
    half = ROPE_DIMS // 2
    first_half = lax.broadcasted_iota(jnp.int32, (tm, cw), 1) % HEAD_DIM < half

    def rotary(t):
        partner = jnp.where(first_half, pltpu.roll(t, cw - half, axis=1),
                            pltpu.roll(t, half, axis=1))
        return t * cos + partner * sin

    for idx in range(3):
        for c in range(DIL_WIDTH // cw):
            col = idx * DIL_WIDTH + c * cw
            z = _dot(h, w_ref[:, col:col + cw])
            if idx < 2:
                z = rotary(z)
            if idx == 0:
                z = z * (LOG2_E * HEAD_DIM ** -0.5)
            outs[idx][:, c * cw:(c + 1) * cw] = z.astype(BF16)
            for j in range(cw // LANES):
                blk = c * (cw // LANES) + j
                perm_refs[0][blk] = z[:, j * LANES:(j + 1) * LANES]
                for gi in range(1, len(DILATED_GROUPS)):
                    d_prev, d = DILATED_GROUPS[gi - 1][1], DILATED_GROUPS[gi][1]
                    ref, src_ref = outs[gi * 3 + idx], perm_refs[gi - 1]
                    rows = tm // d
                    for rho in range(d):
                        start = (rho % d_prev) * (tm // d_prev) + rho // d_prev
                        part = src_ref[blk, pl.ds(start, rows, stride=d // d_prev), :]
                        ref[0, rho, :, blk * LANES:(blk + 1) * LANES] = part.astype(BF16)
                        if gi + 1 < len(DILATED_GROUPS):
                            perm_refs[gi][blk, rho * rows:(rho + 1) * rows, :] = part


def _residue_spec(d, tm, width, tiles_per_batch):
    return pl.BlockSpec((1, d, tm // d, width),
                        lambda i: (i // tiles_per_batch, 0, i % tiles_per_batch, 0))


def _odd_in(x2, g, w, cos, sin, batch):
    n = x2.shape[0]
    t = n // batch
    tm = ROW_TILE
    row = lambda w_: pl.BlockSpec((tm, w_), lambda i: (i, 0))
    out_specs = [row(DIL_WIDTH)] * 3
    out_shape = [jax.ShapeDtypeStruct((n, DIL_WIDTH), BF16)] * 3
    for _, d in DILATED_GROUPS[1:]:
        out_specs += [_residue_spec(d, tm, DIL_WIDTH, t // tm)] * 3
        out_shape += [jax.ShapeDtypeStruct((batch, d, t // d, DIL_WIDTH), BF16)] * 3
    return pl.pallas_call(
        _odd_in_kernel,
        grid=(n // tm,),
        in_specs=[row(D_MODEL), _const_spec((1, D_MODEL)), _const_spec((D_MODEL, 3 * DIL_WIDTH)),
                  row(LANES), row(LANES)],
        out_specs=out_specs,
        out_shape=out_shape,
        scratch_shapes=[pltpu.VMEM((DIL_WIDTH // LANES, tm, LANES), F32)] * (len(DILATED_GROUPS) - 1),
        compiler_params=_params("arbitrary"),
        name="odd_in",
    )(x2, g, w, cos, sin)


def _dilated_kernel(q_ref, kc_ref, kp_ref, vc_ref, vp_ref, o_ref, lse_ref):
    blk = pl.program_id(2)
    sp = DIL_SPAN
    qrows = q_ref.shape[2]
    lane = lax.broadcasted_iota(jnp.int32, (sp, LANES), 1)
    head0 = lane < HEAD_DIM
    qi = lax.broadcasted_iota(jnp.int32, (2 * sp, 2 * sp), 0) % sp
    ki = lax.broadcasted_iota(jnp.int32, (2 * sp, 2 * sp), 1)
    band = (ki >= qi) & (ki <= qi + sp)
    bias = jnp.where(band, 0.0, NEG_INF)
    bias_first = jnp.where(band & (ki >= jnp.where(blk > 0, 0, sp)), 0.0, NEG_INF)
    n_steps = DIL_WIDTH // LANES // DIL_PAIRS_PER_STEP

    for sub in range(qrows // sp):
        r0 = sub * sp

        def step(g, lse_acc, r0=r0, sub=sub):
            for pp in range(DIL_PAIRS_PER_STEP):
                p = g * DIL_PAIRS_PER_STEP + pp
                cols = pl.ds(pl.multiple_of(p * LANES, LANES), LANES)
                q = q_ref[0, 0, r0:r0 + sp, cols]
                if sub == 0:
                    kb = jnp.concatenate([kp_ref[0, 0, qrows - sp:qrows, cols],
                                          kc_ref[0, 0, 0:sp, cols]], axis=0)
                    vb = jnp.concatenate([vp_ref[0, 0, qrows - sp:qrows, cols],
                                          vc_ref[0, 0, 0:sp, cols]], axis=0)
                else:
                    kb = kc_ref[0, 0, r0 - sp:r0 + sp, cols]
                    vb = vc_ref[0, 0, r0 - sp:r0 + sp, cols]
                zero = jnp.zeros_like(q)
                q2 = jnp.concatenate([jnp.where(head0, q, zero), jnp.where(head0, zero, q)], axis=0)
                s = _dot_nt(q2, kb) + (bias_first if sub == 0 else bias)
                m = jnp.max(s, axis=1, keepdims=True)
                e = jnp.exp2(s - m)
                den = jnp.sum(e, axis=1, keepdims=True)
                o2 = _dot(e.astype(BF16), vb) * (1.0 / den)
                lse = m + jnp.log2(den)
                o_ref[0, 0, r0:r0 + sp, cols] = jnp.where(head0, o2[:sp], o2[sp:]).astype(o_ref.dtype)
                lse_acc = jnp.where(lane == 2 * p, lse[:sp],
                                    jnp.where(lane == 2 * p + 1, lse[sp:], lse_acc))
            return lse_acc

        lse_all = lax.fori_loop(0, n_steps, step, jnp.zeros((sp, LANES), F32))
        lse_ref[0, 0, r0:r0 + sp, :] = lse_all


def _dilated_branch(q, k, v, batch):
    _, dilation, length, _ = q.shape
    qrows = min(DIL_QROWS, length)
    nb = length // qrows
    cur = pl.BlockSpec((1, 1, qrows, DIL_WIDTH), lambda b, r, i: (b, r, i, 0))
    prev = pl.BlockSpec((1, 1, qrows, DIL_WIDTH), lambda b, r, i: (b, r, jnp.maximum(i - 1, 0), 0))
    return pl.pallas_call(
        _dilated_kernel,
        grid=(batch, dilation, nb),
        in_specs=[cur, cur, prev, cur, prev],
        out_specs=[cur, pl.BlockSpec((1, 1, qrows, LANES), lambda b, r, i: (b, r, i, 0))],
        out_shape=[jax.ShapeDtypeStruct((batch, dilation, length, DIL_WIDTH), BF16),
                   jax.ShapeDtypeStruct((batch, dilation, length, LANES), F32)],
        compiler_params=_params("arbitrary", "arbitrary", "arbitrary"),
        name=f"dilated_d{dilation}",
    )(q, k, k, v, v)


def _gelu_tanh(x):
    return 0.5 * x * (1.0 + jnp.tanh(0.7978845608028654 * (x + 0.044715 * x * x * x)))


def _finish_layer(m, x_ref, gmp_ref, gfp_ref, gfo_ref, wup_ref, cw_ref, cb_ref, wdn_ref, out_ref,
                  tail_ref, act_ref, tiles_per_batch):
    i = pl.program_id(0)
    tm = x_ref.shape[0]
    x1 = x_ref[...] + _rms_norm(m, gmp_ref[...])
    h = _rms_norm(x1, gfp_ref[...]).astype(BF16)

    @pl.when(i % tiles_per_batch == 0)
    def _():
        tail_ref[...] = jnp.zeros_like(tail_ref)

    ck = FFN_CHUNK
    rows = lax.broadcasted_iota(jnp.int32, (tm, ck), 0)

    def conv(col):
        u = _dot(h, wup_ref[:, col:col + ck])
        t1 = tail_ref[7:8, col:col + ck]
        t2 = tail_ref[6:7, col:col + ck]
        u1 = jnp.where(rows == 0, t1, pltpu.roll(u, 1, axis=0))
        u2 = jnp.where(rows == 0, t2, jnp.where(rows == 1, t1, pltpu.roll(u, 2, axis=0)))
        tail_ref[:, col:col + ck] = u[tm - 8:tm, :]
        return (cb_ref[:, col:col + ck] + cw_ref[2:3, col:col + ck] * u
                + cw_ref[1:2, col:col + ck] * u1 + cw_ref[0:1, col:col + ck] * u2)

    for c in range(D_FF // ck):
        gate = conv(c * ck)
        val = conv(D_FF + c * ck)
        act_ref[:, c * ck:(c + 1) * ck] = (_gelu_tanh(gate) * val).astype(BF16)
    f = _dot(act_ref[...], wdn_ref[...])
    out_ref[...] = x1 + _rms_norm(f, gfo_ref[...])


def _even_post_kernel(yft_ref, yr_ref, x_ref, wo_ref, *rest, tiles_per_batch):
    m = _dot_tn(yft_ref[...], wo_ref[0:FOX_WIDTH, :]) + _dot(yr_ref[...], wo_ref[FOX_WIDTH:, :])
    _finish_layer(m, x_ref, *rest, tiles_per_batch=tiles_per_batch)


def _time_order(src_ref, perm_ref):
    _, d, rows, width = src_ref.shape
    if d == 1:
        return src_ref[0, 0].astype(F32)
    blocks = width // LANES
    for rho in range(d):
        for c in range(blocks):
            perm_ref[c, pl.ds(rho, rows, stride=d), :] = src_ref[
                0, rho, :, c * LANES:(c + 1) * LANES].astype(F32)
    return jnp.concatenate([perm_ref[c] for c in range(blocks)], axis=1)


def _odd_post_kernel(*refs, tiles_per_batch):
    ng = len(DILATED_GROUPS)
    o_refs, l_refs = refs[:ng], refs[ng:2 * ng]
    x_ref, wo_ref = refs[2 * ng:2 * ng + 2]
    rest, (perm_o_ref, perm_l_ref) = refs[2 * ng + 2:-2], refs[-2:]
    lses = [_time_order(l_ref, perm_l_ref) for l_ref in l_refs]
    m = functools.reduce(jnp.maximum, lses)
    es = [jnp.exp2(l - m) for l in lses]
    inv = 1.0 / functools.reduce(jnp.add, es)
    hr = lax.broadcasted_iota(jnp.int32, (LANES, DIL_WIDTH), 0)
    hc = lax.broadcasted_iota(jnp.int32, (LANES, DIL_WIDTH), 1) // HEAD_DIM
    expand = jnp.where(hr == hc, 1.0, 0.0).astype(BF16)
    o = None
    for e, o_ref in zip(es, o_refs):
        term = _dot_exact_rhs(e * inv, expand, terms=1) * _time_order(o_ref, perm_o_ref)
        o = term if o is None else o + term
    _finish_layer(_dot(o.astype(BF16), wo_ref[...]), x_ref, *rest, tiles_per_batch=tiles_per_batch)


def _post_ffn(mixer_outs, x2, wo, g_mix_post, g_ffn_pre, g_ffn_post, w_up, conv_w, conv_b, w_dn,
              batch, even):
    n = x2.shape[0]
    tm = ROW_TILE
    tpb = n // batch // tm
    row = lambda w: pl.BlockSpec((tm, w), lambda i: (i, 0))
    vec = _const_spec((1, D_MODEL))
    def spec(a):
        if a.ndim == 4:
            return _residue_spec(a.shape[1], tm, a.shape[3], tpb)
        if a.shape[0] == n:
            return row(a.shape[1])
        return pl.BlockSpec((a.shape[0], tm), lambda i: (0, i))

    mixer_specs = [spec(a) for a in mixer_outs]
    body = _even_post_kernel if even else _odd_post_kernel
    scratch = [pltpu.VMEM((8, 2 * D_FF), F32), pltpu.VMEM((tm, D_FF), BF16)]
    if not even:
        scratch += [pltpu.VMEM((DIL_WIDTH // LANES, tm, LANES), F32), pltpu.VMEM((1, tm, LANES), F32)]
    return pl.pallas_call(
        functools.partial(body, tiles_per_batch=tpb),
        grid=(n // tm,),
        in_specs=mixer_specs + [row(D_MODEL), _const_spec((D_MODEL, D_MODEL)), vec, vec, vec,
                                _const_spec((D_MODEL, 2 * D_FF)), _const_spec((8, 2 * D_FF)),
                                _const_spec((1, 2 * D_FF)), _const_spec((D_FF, D_MODEL))],
        out_specs=row(D_MODEL),
        out_shape=jax.ShapeDtypeStruct((n, D_MODEL), F32),
        scratch_shapes=scratch,
        compiler_params=_params("arbitrary"),
        name="post_ffn_even" if even else "post_ffn_odd",
    )(*mixer_outs, x2, wo, g_mix_post, g_ffn_pre, g_ffn_post, w_up, conv_w, conv_b, w_dn)


def _pad_rows(a, rows, before=0):
    return jnp.pad(a, ((before, rows - a.shape[0] - before), (0, 0)))


def kernel(x, positions, norm_mix_pre, norm_mix_post, norm_ffn_pre, norm_ffn_post, even_w_in, fox_forget_bias, rwkv_mu, rwkv_w0, rwkv_w2, rwkv_a0, rwkv_a2, rwkv_g2, rwkv_k_k, rwkv_k_a, rwkv_r_k, rwkv_ln_w, rwkv_ln_b, even_w_out, odd_w_in, odd_w_out, ffn_w_up, ffn_conv_w, ffn_conv_b, ffn_w_down):
    batch, seq, _ = x.shape
    n = batch * seq
    depth = norm_mix_pre.shape[0]
    x2 = x.reshape(n, D_MODEL)
    rope = None
    vec = lambda a: a.reshape(1, -1)
    for layer in range(depth):
        i = layer // 2
        g_pre = vec(norm_mix_pre[layer])
        if layer % 2 == 0:
            w_in = even_w_in[i]
            f0 = 3 * FOX_WIDTH
            w_pack = jnp.concatenate(
                [w_in[:, FOX_WIDTH:2 * FOX_WIDTH],
                 jnp.pad(w_in[:, f0:FOX_IN], ((0, 0), (0, LANES - N_FOX_HEADS))),
                 w_in[:, FOX_IN:]], axis=1).astype(BF16)
            w_qv_t = jnp.concatenate([w_in[:, :FOX_WIDTH], w_in[:, 2 * FOX_WIDTH:f0]],
                                     axis=1).T.astype(BF16)
            fb = jnp.pad(fox_forget_bias[i], (0, LANES - N_FOX_HEADS)).reshape(1, LANES)
            qt, kx, vt, ct, rz = _even_in(x2, g_pre, w_pack, w_qv_t, fb, batch)
            y_fox = _fox_attention(qt, kx, vt, ct, batch)
            w2p = _pad_rows(rwkv_w2[i], LANES).astype(BF16)
            a2p = _pad_rows(rwkv_a2[i], LANES, before=DECAY_LORA).astype(BF16)
            y_rwkv = _rwkv_mix(rz, vec(rwkv_mu[i]), vec(rwkv_w0[i]), w2p, vec(rwkv_a0[i]), a2p,
                               rwkv_g2[i].astype(BF16), vec(rwkv_k_k[i]), vec(rwkv_k_a[i]),
                               vec(rwkv_r_k[i]), vec(rwkv_ln_w[i]), vec(rwkv_ln_b[i]), batch)
            mixer_outs = [y_fox, y_rwkv]
            w_out = even_w_out[i]
        else:
            if rope is None:
                rope = _rope_tables(positions)
            qkv = _odd_in(x2, g_pre, odd_w_in[i].astype(BF16), *rope, batch)
            outs, lses = [], []
            for gi, (_, d) in enumerate(DILATED_GROUPS):
                q, k, v = [a.reshape(batch, d, -1, DIL_WIDTH) for a in qkv[3 * gi:3 * gi + 3]]
                o, lse = _dilated_branch(q, k, v, batch)
                outs.append(o)
                lses.append(lse)
            mixer_outs = outs + lses
            w_out = odd_w_out[i]
        x2 = _post_ffn(mixer_outs, x2, w_out.astype(BF16), vec(norm_mix_post[layer]),
                       vec(norm_ffn_pre[layer]), vec(norm_ffn_post[layer]),
                       ffn_w_up[layer].astype(BF16), _pad_rows(ffn_conv_w[layer], 8),
                       vec(ffn_conv_b[layer]), ffn_w_down[layer].astype(BF16), batch,
                       even=layer % 2 == 0)
    return x2.reshape(batch, seq, D_MODEL)
```

```python
import functools

import jax
import jax.numpy as jnp
from jax import lax
from jax.experimental import pallas as pl
from jax.experimental.pallas import tpu as pltpu

F32 = jnp.float32
BF16 = jnp.bfloat16

D_MODEL = 1024
HEAD_DIM = 64
N_FOX_HEADS = 8
FOX_WIDTH = 512
RWKV_WIDTH = 512
DECAY_LORA = 64
ICLR_LORA = 64
GATE_LORA = 128
RWKV_IN = 3 * RWKV_WIDTH + DECAY_LORA + ICLR_LORA + GATE_LORA
FOX_IN = 3 * FOX_WIDTH + N_FOX_HEADS
N_DIL_HEADS = 16
DIL_WIDTH = 1024
DILATED_GROUPS = ((128, 1), (512, 4), (2048, 16))
DIL_SPAN = 128
ROPE_THETA = 500000.0
ROPE_DIMS = 16
D_FF = 2816
RMS_EPS = 1e-6
GN_EPS = 64e-5
NEG_INF = -1e30
LOG2_E = 1.4426950408889634

LANES = 128
V7X_VMEM_LIMIT = 56 * 1024 * 1024

ROW_TILE = 512
FOX_TILE = 1024
RWKV_TILE = 256
RWKV_CHUNK = 64
RWKV_GROUP = 256
RWKV_SUM_TERMS = 1
DIL_QROWS = 512
DIL_PAIRS_PER_STEP = 8
FFN_CHUNK = 256


def _dot(a, b):
    return jnp.dot(a, b, preferred_element_type=F32)


def _dot_nt(a, b):
    return lax.dot_general(a, b, (((1,), (1,)), ((), ())), preferred_element_type=F32)


def _dot_tn(a, b):
    return lax.dot_general(a, b, (((0,), (0,)), ((), ())), preferred_element_type=F32)


def _split(x, terms):
    parts = []
    for _ in range(terms):
        part = x.astype(BF16)
        parts.append(part)
        x = x - part.astype(F32)
    return parts


def _dot_exact_lhs(sel, x, terms=3):
    return functools.reduce(jnp.add, [_dot(sel, part) for part in _split(x, terms)])


def _dot_exact_rhs(x, sel, terms=3):
    return functools.reduce(jnp.add, [_dot(part, sel) for part in _split(x, terms)])


def _rms_norm(x, g):
    return x * lax.rsqrt(jnp.mean(x * x, axis=-1, keepdims=True) + RMS_EPS) * g


def _softplus(x):
    return jnp.maximum(x, 0.0) + jnp.log(1.0 + jnp.exp(-jnp.abs(x)))


def _sigmoid(x):
    return 1.0 / (1.0 + jnp.exp(-x))


def _const_spec(shape):
    nd = len(shape)
    return pl.BlockSpec(shape, lambda *_: (0,) * nd, pipeline_mode=pl.Buffered(1))


def _params(*sem):
    return pltpu.CompilerParams(dimension_semantics=sem, vmem_limit_bytes=V7X_VMEM_LIMIT)


EVEN_PACKED = FOX_WIDTH + LANES + RWKV_IN

FOX_PAIRS = FOX_WIDTH // LANES
FOX_KEY_LANES = 2 * LANES
FOX_BIAS_TERMS = 3


def _fox_key_bias_selectors():
    r = lax.broadcasted_iota(jnp.int32, (LANES, FOX_PAIRS * LANES), 0)
    cidx = lax.broadcasted_iota(jnp.int32, (LANES, FOX_PAIRS * LANES), 1)
    pair, slot = cidx // LANES, cidx % LANES
    return [jnp.where((r // 2 == pair) & (r < N_FOX_HEADS)
                      & (slot == FOX_BIAS_TERMS * (r % 2) + t), 1.0, 0.0).astype(BF16)
            for t in range(FOX_BIAS_TERMS)]


def _even_in_kernel(x_ref, g_ref, w_ref, wt_ref, fb_ref, qt_ref, kx_ref, vt_ref, ct_ref, rz_ref,
                    carry_ref, *, tiles_per_batch):
    i = pl.program_id(0)
    tm = x_ref.shape[0]
    h = _rms_norm(x_ref[...], g_ref[...]).astype(BF16)
    qt_ref[...] = (_dot_nt(wt_ref[0:FOX_WIDTH, :], h) * (LOG2_E * HEAD_DIM ** -0.5)).astype(BF16)
    vt_ref[...] = _dot_nt(wt_ref[FOX_WIDTH:, :], h).astype(BF16)
    f = _dot(h, w_ref[:, FOX_WIDTH:FOX_WIDTH + LANES]) + fb_ref[...]
    log2_f = -_softplus(-f) * LOG2_E

    @pl.when(i % tiles_per_batch == 0)
    def _():
        carry_ref[...] = jnp.zeros_like(carry_ref)

    row = lax.broadcasted_iota(jnp.int32, (tm, tm), 0)
    col = lax.broadcasted_iota(jnp.int32, (tm, tm), 1)
    tri = jnp.where(row >= col, 1.0, 0.0).astype(BF16)
    c = _dot_exact_lhs(tri, log2_f) + carry_ref[0:1, :]
    carry_ref[...] = jnp.broadcast_to(c[tm - 1:tm, :], carry_ref.shape)
    ct_ref[...] = jnp.transpose(c)[0:N_FOX_HEADS, :]

    k = _dot(h, w_ref[:, 0:FOX_WIDTH])
    bias = functools.reduce(jnp.add, [_dot(part, sel) for part, sel in
                                      zip(_split(-c, FOX_BIAS_TERMS), _fox_key_bias_selectors())])
    slot = lax.broadcasted_iota(jnp.int32, (tm, FOX_PAIRS * LANES), 1) % LANES
    ones = (slot >= 2 * FOX_BIAS_TERMS) & (slot < 3 * FOX_BIAS_TERMS)
    bias = jnp.where(ones, 1.0, bias).astype(BF16)
    for p in range(FOX_PAIRS):
        kx_ref[:, p * FOX_KEY_LANES:p * FOX_KEY_LANES + LANES] = (
            k[:, p * LANES:(p + 1) * LANES].astype(BF16))
        kx_ref[:, p * FOX_KEY_LANES + LANES:(p + 1) * FOX_KEY_LANES] = (
            bias[:, p * LANES:(p + 1) * LANES])
    rz_ref[...] = _dot(h, w_ref[:, FOX_WIDTH + LANES:])


def _even_in(x2, g, w_pack, w_qv_t, fb, batch):
    n = x2.shape[0]
    tm = ROW_TILE
    row = lambda w: pl.BlockSpec((tm, w), lambda i: (i, 0))
    col = lambda h: pl.BlockSpec((h, tm), lambda i: (0, i))
    return pl.pallas_call(
        functools.partial(_even_in_kernel, tiles_per_batch=n // batch // tm),
        grid=(n // tm,),
        in_specs=[row(D_MODEL), _const_spec((1, D_MODEL)), _const_spec((D_MODEL, EVEN_PACKED)),
                  _const_spec((2 * FOX_WIDTH, D_MODEL)), _const_spec((1, LANES))],
        out_specs=[col(FOX_WIDTH), row(FOX_PAIRS * FOX_KEY_LANES), col(FOX_WIDTH),
                   col(N_FOX_HEADS), row(RWKV_IN)],
        out_shape=[jax.ShapeDtypeStruct((FOX_WIDTH, n), BF16),
                   jax.ShapeDtypeStruct((n, FOX_PAIRS * FOX_KEY_LANES), BF16),
                   jax.ShapeDtypeStruct((FOX_WIDTH, n), BF16),
                   jax.ShapeDtypeStruct((N_FOX_HEADS, n), F32),
                   jax.ShapeDtypeStruct((n, RWKV_IN), F32)],
        scratch_shapes=[pltpu.VMEM((8, LANES), F32)],
        compiler_params=_params("arbitrary"),
        name="even_in",
    )(x2, g, w_pack, w_qv_t, fb)


def _fox_kernel(qt_ref, kx_ref, vt_ref, ct_ref, o_ref, s_ref):
    hp = pl.program_id(1)
    qi = pl.program_id(2)
    tq = qt_ref.shape[1]
    qt = qt_ref[...]
    row = lax.broadcasted_iota(jnp.int32, (LANES, tq), 0)
    head0 = row < HEAD_DIM
    zero = jnp.zeros_like(qt)

    def bias_rows(hh):
        c = ct_ref[pl.ds(2 * hp + hh, 1), :]
        terms = [part.astype(F32) for part in _split(c, FOX_BIAS_TERMS)]
        b = jnp.zeros((LANES, tq), F32)
        for t, term in enumerate(terms):
            b = jnp.where(row == 2 * FOX_BIAS_TERMS + t, term, b)
        own = (row >= FOX_BIAS_TERMS * hh) & (row < FOX_BIAS_TERMS * (hh + 1))
        return jnp.where(own, 1.0, b).astype(BF16)

    qxt = jnp.concatenate([
        jnp.concatenate([jnp.where(head0, qt, zero), bias_rows(0)], axis=0),
        jnp.concatenate([jnp.where(head0, zero, qt), bias_rows(1)], axis=0)], axis=1)
    tk = tq
    key_idx = lax.broadcasted_iota(jnp.int32, (tk, 2 * tq), 0)
    query_idx = lax.broadcasted_iota(jnp.int32, (tk, 2 * tq), 1) % tq
    causal = key_idx <= query_idx

    def step(j, carry, masked):
        m, l, acc = carry
        k0 = pl.multiple_of(j * tk, tk)
        s = _dot(kx_ref[pl.ds(k0, tk), :], qxt)
        if masked:
            s = jnp.where(causal, s, NEG_INF)
        s_ref[...] = s
        m_new = jnp.maximum(m, jnp.max(s_ref[...], axis=0, keepdims=True))
        p = jnp.exp2(s_ref[...] - m_new)
        alpha = jnp.exp2(m - m_new)
        l = alpha * l + jnp.sum(p, axis=0, keepdims=True)
        acc = alpha * acc + _dot(vt_ref[:, pl.ds(k0, tk)], p.astype(BF16))
        return m_new, l, acc

    init = (jnp.full((1, 2 * tq), NEG_INF, F32), jnp.zeros((1, 2 * tq), F32),
            jnp.zeros((LANES, 2 * tq), F32))
    carry = lax.fori_loop(0, qi, lambda j, c: step(j, c, False), init)
    _, l, acc = step(qi, carry, True)
    out = acc / l
    o_ref[...] = jnp.where(head0, out[:, :tq], out[:, tq:]).astype(BF16)


def _fox_attention(qt, kx, vt, ct, batch):
    n = qt.shape[1]
    t = n // batch
    tq = FOX_TILE
    nq = t // tq
    return pl.pallas_call(
        _fox_kernel,
        grid=(batch, FOX_PAIRS, nq),
        in_specs=[
            pl.BlockSpec((LANES, tq), lambda b, p, i: (p, b * nq + i)),
            pl.BlockSpec((t, FOX_KEY_LANES), lambda b, p, i: (b, p)),
            pl.BlockSpec((LANES, t), lambda b, p, i: (p, b)),
            pl.BlockSpec((N_FOX_HEADS, tq), lambda b, p, i: (0, b * nq + i)),
        ],
        out_specs=pl.BlockSpec((LANES, tq), lambda b, p, i: (p, b * nq + i)),
        out_shape=jax.ShapeDtypeStruct((FOX_WIDTH, n), BF16),
        scratch_shapes=[pltpu.VMEM((tq, 2 * tq), F32)],
        compiler_params=_params("arbitrary", "arbitrary", "arbitrary"),
        name="fox_attention",
    )(qt, kx, vt, ct)


def _rwkv_kernel(rz_ref, mu_ref, w0_ref, w2_ref, a0_ref, a2_ref, g2_ref, kk_ref, ka_ref, rk_ref,
                 lnw_ref, lnb_ref, y_ref, state_ref, tail_ref, ybuf_ref):
    tc = rz_ref.shape[0]
    ch = RWKV_CHUNK
    gw = RWKV_GROUP
    n_groups = RWKV_WIDTH // gw

    @pl.when(pl.program_id(1) == 0)
    def _():
        state_ref[...] = jnp.zeros_like(state_ref)
        tail_ref[...] = jnp.zeros_like(tail_ref)

    z = rz_ref[...]
    rows = lax.broadcasted_iota(jnp.int32, z.shape, 0)
    z_prev = jnp.where(rows == 0, tail_ref[7:8, :], pltpu.roll(z, 1, axis=0))
    tail_ref[...] = z[tc - 8:tc, :]
    z = z + (z_prev - z) * mu_ref[...]
    r = z[:, 0:RWKV_WIDTH]
    k = z[:, RWKV_WIDTH:2 * RWKV_WIDTH]
    v = z[:, 2 * RWKV_WIDTH:3 * RWKV_WIDTH]
    lo = z[:, 3 * RWKV_WIDTH:3 * RWKV_WIDTH + LANES]
    g_lo = z[:, 3 * RWKV_WIDTH + LANES:]

    w = w0_ref[...] + _dot(jnp.tanh(lo).astype(BF16), w2_ref[...])
    w = -_softplus(-w) - 0.5
    log2_decay = -jnp.exp(w) * LOG2_E
    a = _sigmoid(a0_ref[...] + _dot(lo.astype(BF16), a2_ref[...]))
    gate = _dot(_sigmoid(g_lo).astype(BF16), g2_ref[...])

    br = lax.broadcasted_iota(jnp.int32, (gw, gw), 0) // HEAD_DIM
    bc = lax.broadcasted_iota(jnp.int32, (gw, gw), 1) // HEAD_DIM
    same_head = br == bc
    head_ones = jnp.where(same_head, 1.0, 0.0).astype(BF16)

    def seg_sum(t):
        return jnp.concatenate(
            [_dot_exact_rhs(t[:, g * gw:(g + 1) * gw], head_ones, terms=RWKV_SUM_TERMS)
             for g in range(n_groups)], axis=1)

    kk = k * kk_ref[...]
    kk = kk / jnp.maximum(jnp.sqrt(seg_sum(kk * kk)), 1e-12)
    k = k * (1.0 + (a - 1.0) * ka_ref[...])
    aa = -kk
    bb = kk * a
    bonus = seg_sum(r * k * rk_ref[...]) * v

    tr = lax.broadcasted_iota(jnp.int32, (ch, ch), 0)
    ts = lax.broadcasted_iota(jnp.int32, (ch, ch), 1)
    tri_incl = jnp.where(tr >= ts, 1.0, 0.0).astype(BF16)
    gr = lax.broadcasted_iota(jnp.int32, (ch, gw), 0)
    gs = lax.broadcasted_iota(jnp.int32, (ch, gw), 1) % ch
    lower = gr >= gs
    strict = gr > gs
    eye = jnp.where(gr == gs, 1.0, 0.0)

    def block_diag(t):
        tiled = jnp.concatenate([t] * (gw // ch), axis=0)
        return jnp.where(same_head, tiled, 0.0).astype(BF16)

    stack = lambda x, y: jnp.concatenate([x, y], axis=0).astype(BF16)
    n_chunks = tc // ch
    chains = [(c, grp) for c in range(n_chunks) for grp in range(n_groups)]
    cs = {}
    for c in range(n_chunks):
        sl = slice(c * ch, (c + 1) * ch)
        ld = log2_decay[sl]
        cum = _dot_exact_lhs(tri_incl, ld)
        cum_end = cum[ch - 1:ch, :]
        p_inv = jnp.exp2(-cum)
        p_rest = jnp.exp2(cum_end - cum)
        cs[c] = dict(p_end=jnp.exp2(cum_end), a_t=aa[sl] * jnp.exp2(cum - ld),
                     r_t=r[sl] * jnp.exp2(cum), b_t=bb[sl] * p_inv, k_t=k[sl] * p_inv,
                     b_h=bb[sl] * p_rest, k_h=k[sl] * p_rest, v=v[sl])
    st = {}
    for c, grp in chains:
        gl = slice(grp * gw, (grp + 1) * gw)
        d = {name: val[:, gl] for name, val in cs[c].items()}
        ar = stack(d["a_t"], d["r_t"])
        sb = _dot_nt(ar, block_diag(d["b_t"]))
        sk = _dot_nt(ar, block_diag(d["k_t"]))
        d["a_ab"] = jnp.where(strict, sb[:ch], 0.0)
        d["a_rb"] = jnp.where(lower, sb[ch:], 0.0)
        a_ak = jnp.where(strict, sk[:ch], 0.0)
        a_rk = jnp.where(lower, sk[ch:], 0.0)
        av = _dot(stack(a_ak, a_rk), block_diag(d["v"]))
        d["av"], d["rv"] = av[:ch], av[ch:]
        d["inv"] = eye + d["a_ab"]
        d["power"] = d["a_ab"]
        st[c, grp] = d
    levels = ch.bit_length() - 1
    for lvl in range(levels):
        for key in chains:
            d = st[key]
            rhs = block_diag(d["power"])
            if lvl == 0:
                d["power"] = _dot(d["power"].astype(BF16), rhs)
            elif lvl < levels - 1:
                both = _dot(stack(d["power"], d["inv"]), rhs)
                d["power"] = both[:ch]
                d["inv"] = d["inv"] + both[ch:]
            else:
                d["inv"] = d["inv"] + _dot(d["inv"].astype(BF16), rhs)
    for key in chains:
        d = st[key]
        inv = d["inv"].astype(BF16)
        d["ta"] = _dot(inv, block_diag(d["a_t"]))
        d["u0"] = _dot(inv, block_diag(d["av"]))
    for key in chains:
        d = st[key]
        a_rb = d["a_rb"].astype(BF16)
        d["query"] = d["r_t"] + _dot(a_rb, block_diag(d["ta"]))
        d["y0"] = d["rv"] + _dot(a_rb, block_diag(d["u0"]))
        d["mix"] = jnp.where(same_head, _dot_tn(d["ta"].astype(BF16), d["b_h"].astype(BF16)), 0.0)
        d["add"] = jnp.where(same_head, _dot_tn(stack(d["u0"], d["v"]),
                                                stack(d["b_h"], d["k_h"])), 0.0)
    for grp in range(n_groups):
        gl = slice(grp * gw, (grp + 1) * gw)
        g_state = state_ref[grp]
        for c in range(n_chunks):
            d = st[c, grp]
            g_bf = g_state.astype(BF16)
            ybuf_ref[c * ch:(c + 1) * ch, gl] = _dot_nt(d["query"].astype(BF16), g_bf) + d["y0"]
            g_state = g_state * d["p_end"] + _dot(g_bf, d["mix"].astype(BF16)) + d["add"]
        state_ref[grp] = g_state

    y = ybuf_ref[...]
    inv_n = 1.0 / HEAD_DIM
    mean = seg_sum(y) * inv_n
    d = y - mean
    var = seg_sum(d * d) * inv_n
    yn = d * lax.rsqrt(var + GN_EPS) * lnw_ref[...] + lnb_ref[...]
    y_ref[...] = ((yn + bonus) * gate).astype(BF16)


def _rwkv_mix(rz, mu, w0, w2p, a0, a2p, g2, k_k, k_a, r_k, ln_w, ln_b, batch):
    n = rz.shape[0]
    t = n // batch
    tc = RWKV_TILE
    nt = t // tc
    vec = lambda w: _const_spec((1, w))
    return pl.pallas_call(
        _rwkv_kernel,
        grid=(batch, nt),
        in_specs=[pl.BlockSpec((tc, RWKV_IN), lambda b, i: (b * nt + i, 0)),
                  vec(RWKV_IN), vec(RWKV_WIDTH), _const_spec((LANES, RWKV_WIDTH)),
                  vec(RWKV_WIDTH), _const_spec((LANES, RWKV_WIDTH)),
                  _const_spec((GATE_LORA, RWKV_WIDTH)),
                  vec(RWKV_WIDTH), vec(RWKV_WIDTH), vec(RWKV_WIDTH), vec(RWKV_WIDTH),
                  vec(RWKV_WIDTH)],
        out_specs=pl.BlockSpec((tc, RWKV_WIDTH), lambda b, i: (b * nt + i, 0)),
        out_shape=jax.ShapeDtypeStruct((n, RWKV_WIDTH), BF16),
        scratch_shapes=[pltpu.VMEM((RWKV_WIDTH // RWKV_GROUP, RWKV_GROUP, RWKV_GROUP), F32),
                        pltpu.VMEM((8, RWKV_IN), F32),
                        pltpu.VMEM((tc, RWKV_WIDTH), F32)],
        compiler_params=_params("arbitrary", "arbitrary"),
        name="rwkv_mix",
    )(rz, mu, w0, w2p, a0, a2p, g2, k_k, k_a, r_k, ln_w, ln_b)


def _rope_table_kernel(pos_ref, freq_ref, cos_ref, sin_ref):
    ang = pos_ref[...] * freq_ref[...]
    s = jnp.sin(ang)
    d = lax.broadcasted_iota(jnp.int32, ang.shape, 1) % HEAD_DIM
    half = ROPE_DIMS // 2
    cos_ref[...] = jnp.where(d < ROPE_DIMS, jnp.cos(ang), 1.0)
    sin_ref[...] = jnp.where(d < half, -s, jnp.where(d < ROPE_DIMS, s, 0.0))


def _rope_tables(positions):
    n = positions.size
    tm = 2048
    pos = jnp.broadcast_to(positions.reshape(n, 1).astype(F32), (n, LANES))
    half = ROPE_DIMS // 2
    inv_freq = ROPE_THETA ** (-jnp.arange(0, ROPE_DIMS, 2, dtype=F32) / ROPE_DIMS)
    d = jnp.arange(LANES) % HEAD_DIM
    freq = jnp.where(d < ROPE_DIMS, inv_freq[d % half], 0.0).reshape(1, LANES)
    row = pl.BlockSpec((tm, LANES), lambda i: (i, 0))
    return pl.pallas_call(
        _rope_table_kernel,
        grid=(n // tm,),
        in_specs=[row, _const_spec((1, LANES))],
        out_specs=[row, row],
        out_shape=[jax.ShapeDtypeStruct((n, LANES), F32)] * 2,
        compiler_params=_params("arbitrary"),
        name="rope_tables",
    )(pos, freq)


def _odd_in_kernel(x_ref, g_ref, w_ref, cos_ref, sin_ref, *refs):
    n_perm = len(DILATED_GROUPS) - 1
    outs, perm_refs = refs[:-n_perm], refs[-n_perm:]
    tm = x_ref.shape[0]
    h = _rms_norm(x_ref[...], g_ref[...]).astype(BF16)
    cw = 2 * LANES
    cos = jnp.concatenate([cos_ref[...]] * (cw // LANES), axis=1)
    sin = jnp.concatenate([sin_ref[...]] * (cw // LANES), axis=1)
    half = ROPE_DIMS // 2
    first_half = lax.broadcasted_iota(jnp.int32, (tm, cw), 1) % HEAD_DIM < half

    def rotary(t):
        partner = jnp.where(first_half, pltpu.roll(t, cw - half, axis=1),
                            pltpu.roll(t, half, axis=1))
        return t * cos + partner * sin

    for idx in range(3):
        for c in range(DIL_WIDTH // cw):
            col = idx * DIL_WIDTH + c * cw
            z = _dot(h, w_ref[:, col:col + cw])
            if idx < 2:
                z = rotary(z)
            if idx == 0:
                z = z * (LOG2_E * HEAD_DIM ** -0.5)
            outs[idx][:, c * cw:(c + 1) * cw] = z.astype(BF16)
            for j in range(cw // LANES):
                blk = c * (cw // LANES) + j
                perm_refs[0][blk] = z[:, j * LANES:(j + 1) * LANES]
                for gi in range(1, len(DILATED_GROUPS)):
                    d_prev, d = DILATED_GROUPS[gi - 1][1], DILATED_GROUPS[gi][1]
                    ref, src_ref = outs[gi * 3 + idx], perm_refs[gi - 1]
                    rows = tm // d
                    for rho in range(d):
                        start = (rho % d_prev) * (tm // d_prev) + rho // d_prev
                        part = src_ref[blk, pl.ds(start, rows, stride=d // d_prev), :]
                        ref[0, rho, :, blk * LANES:(blk + 1) * LANES] = part.astype(BF16)
                        if gi + 1 < len(DILATED_GROUPS):
                            perm_refs[gi][blk, rho * rows:(rho + 1) * rows, :] = part


def _residue_spec(d, tm, width, tiles_per_batch):
    return pl.BlockSpec((1, d, tm // d, width),
                        lambda i: (i // tiles_per_batch, 0, i % tiles_per_batch, 0))


def _odd_in(x2, g, w, cos, sin, batch):
    n = x2.shape[0]
    t = n // batch
    tm = ROW_TILE
    row = lambda w_: pl.BlockSpec((tm, w_), lambda i: (i, 0))
    out_specs = [row(DIL_WIDTH)] * 3
    out_shape = [jax.ShapeDtypeStruct((n, DIL_WIDTH), BF16)] * 3
    for _, d in DILATED_GROUPS[1:]:
        out_specs += [_residue_spec(d, tm, DIL_WIDTH, t // tm)] * 3
        out_shape += [jax.ShapeDtypeStruct((batch, d, t // d, DIL_WIDTH), BF16)] * 3
    return pl.pallas_call(
        _odd_in_kernel,
        grid=(n // tm,),
        in_specs=[row(D_MODEL), _const_spec((1, D_MODEL)), _const_spec((D_MODEL, 3 * DIL_WIDTH)),
                  row(LANES), row(LANES)],
        out_specs=out_specs,
        out_shape=out_shape,
        scratch_shapes=[pltpu.VMEM((DIL_WIDTH // LANES, tm, LANES), F32)] * (len(DILATED_GROUPS) - 1),
        compiler_params=_params("arbitrary"),
        name="odd_in",
    )(x2, g, w, cos, sin)


def _dilated_kernel(q_ref, kc_ref, kp_ref, vc_ref, vp_ref, o_ref, lse_ref):
    blk = pl.program_id(2)
    sp = DIL_SPAN
    qrows = q_ref.shape[2]
    lane = lax.broadcasted_iota(jnp.int32, (sp, LANES), 1)
    head0 = lane < HEAD_DIM
    qi = lax.broadcasted_iota(jnp.int32, (2 * sp, 2 * sp), 0) % sp
    ki = lax.broadcasted_iota(jnp.int32, (2 * sp, 2 * sp), 1)
    band = (ki >= qi) & (ki <= qi + sp)
    bias = jnp.where(band, 0.0, NEG_INF)
    bias_first = jnp.where(band & (ki >= jnp.where(blk > 0, 0, sp)), 0.0, NEG_INF)
    n_steps = DIL_WIDTH // LANES // DIL_PAIRS_PER_STEP

    for sub in range(qrows // sp):
        r0 = sub * sp

        def step(g, lse_acc, r0=r0, sub=sub):
            for pp in range(DIL_PAIRS_PER_STEP):
                p = g * DIL_PAIRS_PER_STEP + pp
                cols = pl.ds(pl.multiple_of(p * LANES, LANES), LANES)
                q = q_ref[0, 0, r0:r0 + sp, cols]
                if sub == 0:
                    kb = jnp.concatenate([kp_ref[0, 0, qrows - sp:qrows, cols],
                                          kc_ref[0, 0, 0:sp, cols]], axis=0)
                    vb = jnp.concatenate([vp_ref[0, 0, qrows - sp:qrows, cols],
                                          vc_ref[0, 0, 0:sp, cols]], axis=0)
                else:
                    kb = kc_ref[0, 0, r0 - sp:r0 + sp, cols]
                    vb = vc_ref[0, 0, r0 - sp:r0 + sp, cols]
                zero = jnp.zeros_like(q)
                q2 = jnp.concatenate([jnp.where(head0, q, zero), jnp.where(head0, zero, q)], axis=0)
                s = _dot_nt(q2, kb) + (bias_first if sub == 0 else bias)
                m = jnp.max(s, axis=1, keepdims=True)
                e = jnp.exp2(s - m)
                den = jnp.sum(e, axis=1, keepdims=True)
                o2 = _dot(e.astype(BF16), vb) * (1.0 / den)
                lse = m + jnp.log2(den)
                o_ref[0, 0, r0:r0 + sp, cols] = jnp.where(head0, o2[:sp], o2[sp:]).astype(o_ref.dtype)
                lse_acc = jnp.where(lane == 2 * p, lse[:sp],
                                    jnp.where(lane == 2 * p + 1, lse[sp:], lse_acc))
            return lse_acc

        lse_all = lax.fori_loop(0, n_steps, step, jnp.zeros((sp, LANES), F32))
        lse_ref[0, 0, r0:r0 + sp, :] = lse_all


def _dilated_branch(q, k, v, batch):
    _, dilation, length, _ = q.shape
    qrows = min(DIL_QROWS, length)
    nb = length // qrows
    cur = pl.BlockSpec((1, 1, qrows, DIL_WIDTH), lambda b, r, i: (b, r, i, 0))
    prev = pl.BlockSpec((1, 1, qrows, DIL_WIDTH), lambda b, r, i: (b, r, jnp.maximum(i - 1, 0), 0))
    return pl.pallas_call(
        _dilated_kernel,
        grid=(batch, dilation, nb),
        in_specs=[cur, cur, prev, cur, prev],
        out_specs=[cur, pl.BlockSpec((1, 1, qrows, LANES), lambda b, r, i: (b, r, i, 0))],
        out_shape=[jax.ShapeDtypeStruct((batch, dilation, length, DIL_WIDTH), BF16),
                   jax.ShapeDtypeStruct((batch, dilation, length, LANES), F32)],
        compiler_params=_params("arbitrary", "arbitrary", "arbitrary"),
        name=f"dilated_d{dilation}",
    )(q, k, k, v, v)


def _gelu_tanh(x):
    return 0.5 * x * (1.0 + jnp.tanh(0.7978845608028654 * (x + 0.044715 * x * x * x)))


def _finish_layer(m, x_ref, gmp_ref, gfp_ref, gfo_ref, wup_ref, cw_ref, cb_ref, wdn_ref, out_ref,
                  tail_ref, act_ref, tiles_per_batch):
    i = pl.program_id(0)
    tm = x_ref.shape[0]
    x1 = x_ref[...] + _rms_norm(m, gmp_ref[...])
    h = _rms_norm(x1, gfp_ref[...]).astype(BF16)

    @pl.when(i % tiles_per_batch == 0)
    def _():
        tail_ref[...] = jnp.zeros_like(tail_ref)

    ck = FFN_CHUNK
    rows = lax.broadcasted_iota(jnp.int32, (tm, ck), 0)

    def conv(col):
        u = _dot(h, wup_ref[:, col:col + ck])
        t1 = tail_ref[7:8, col:col + ck]
        t2 = tail_ref[6:7, col:col + ck]
        u1 = jnp.where(rows == 0, t1, pltpu.roll(u, 1, axis=0))
        u2 = jnp.where(rows == 0, t2, jnp.where(rows == 1, t1, pltpu.roll(u, 2, axis=0)))
        tail_ref[:, col:col + ck] = u[tm - 8:tm, :]
        return (cb_ref[:, col:col + ck] + cw_ref[2:3, col:col + ck] * u
                + cw_ref[1:2, col:col + ck] * u1 + cw_ref[0:1, col:col + ck] * u2)

    for c in range(D_FF // ck):
        gate = conv(c * ck)
        val = conv(D_FF + c * ck)
        act_ref[:, c * ck:(c + 1) * ck] = (_gelu_tanh(gate) * val).astype(BF16)
    f = _dot(act_ref[...], wdn_ref[...])
    out_ref[...] = x1 + _rms_norm(f, gfo_ref[...])


def _even_post_kernel(yft_ref, yr_ref, x_ref, wo_ref, *rest, tiles_per_batch):
    m = _dot_tn(yft_ref[...], wo_ref[0:FOX_WIDTH, :]) + _dot(yr_ref[...], wo_ref[FOX_WIDTH:, :])
    _finish_layer(m, x_ref, *rest, tiles_per_batch=tiles_per_batch)


def _time_order(src_ref, perm_ref):
    _, d, rows, width = src_ref.shape
    if d == 1:
        return src_ref[0, 0].astype(F32)
    blocks = width // LANES
    for rho in range(d):
        for c in range(blocks):
            perm_ref[c, pl.ds(rho, rows, stride=d), :] = src_ref[
                0, rho, :, c * LANES:(c + 1) * LANES].astype(F32)
    return jnp.concatenate([perm_ref[c] for c in range(blocks)], axis=1)


def _odd_post_kernel(*refs, tiles_per_batch):
    ng = len(DILATED_GROUPS)
    o_refs, l_refs = refs[:ng], refs[ng:2 * ng]
    x_ref, wo_ref = refs[2 * ng:2 * ng + 2]
    rest, (perm_o_ref, perm_l_ref) = refs[2 * ng + 2:-2], refs[-2:]
    lses = [_time_order(l_ref, perm_l_ref) for l_ref in l_refs]
    m = functools.reduce(jnp.maximum, lses)
    es = [jnp.exp2(l - m) for l in lses]
    inv = 1.0 / functools.reduce(jnp.add, es)
    hr = lax.broadcasted_iota(jnp.int32, (LANES, DIL_WIDTH), 0)
    hc = lax.broadcasted_iota(jnp.int32, (LANES, DIL_WIDTH), 1) // HEAD_DIM
    expand = jnp.where(hr == hc, 1.0, 0.0).astype(BF16)
    o = None
    for e, o_ref in zip(es, o_refs):
        term = _dot_exact_rhs(e * inv, expand, terms=1) * _time_order(o_ref, perm_o_ref)
        o = term if o is None else o + term
    _finish_layer(_dot(o.astype(BF16), wo_ref[...]), x_ref, *rest, tiles_per_batch=tiles_per_batch)


def _post_ffn(mixer_outs, x2, wo, g_mix_post, g_ffn_pre, g_ffn_post, w_up, conv_w, conv_b, w_dn,
              batch, even):
    n = x2.shape[0]
    tm = ROW_TILE
    tpb = n // batch // tm
    row = lambda w: pl.BlockSpec((tm, w), lambda i: (i, 0))
    vec = _const_spec((1, D_MODEL))
    def spec(a):
        if a.ndim == 4:
            return _residue_spec(a.shape[1], tm, a.shape[3], tpb)
        if a.shape[0] == n:
            return row(a.shape[1])
        return pl.BlockSpec((a.shape[0], tm), lambda i: (0, i))

    mixer_specs = [spec(a) for a in mixer_outs]
    body = _even_post_kernel if even else _odd_post_kernel
    scratch = [pltpu.VMEM((8, 2 * D_FF), F32), pltpu.VMEM((tm, D_FF), BF16)]
    if not even:
        scratch += [pltpu.VMEM((DIL_WIDTH // LANES, tm, LANES), F32), pltpu.VMEM((1, tm, LANES), F32)]
    return pl.pallas_call(
        functools.partial(body, tiles_per_batch=tpb),
        grid=(n // tm,),
        in_specs=mixer_specs + [row(D_MODEL), _const_spec((D_MODEL, D_MODEL)), vec, vec, vec,
                                _const_spec((D_MODEL, 2 * D_FF)), _const_spec((8, 2 * D_FF)),
                                _const_spec((1, 2 * D_FF)), _const_spec((D_FF, D_MODEL))],
        out_specs=row(D_MODEL),
        out_shape=jax.ShapeDtypeStruct((n, D_MODEL), F32),
        scratch_shapes=scratch,
        compiler_params=_params("arbitrary"),
        name="post_ffn_even" if even else "post_ffn_odd",
    )(*mixer_outs, x2, wo, g_mix_post, g_ffn_pre, g_ffn_post, w_up, conv_w, conv_b, w_dn)


def _pad_rows(a, rows, before=0):
    return jnp.pad(a, ((before, rows - a.shape[0] - before), (0, 0)))


def kernel(x, positions, norm_mix_pre, norm_mix_post, norm_ffn_pre, norm_ffn_post, even_w_in, fox_forget_bias, rwkv_mu, rwkv_w0, rwkv_w2, rwkv_a0, rwkv_a2, rwkv_g2, rwkv_k_k, rwkv_k_a, rwkv_r_k, rwkv_ln_w, rwkv_ln_b, even_w_out, odd_w_in, odd_w_out, ffn_w_up, ffn_conv_w, ffn_conv_b, ffn_w_down):
    batch, seq, _ = x.shape
    n = batch * seq
    depth = norm_mix_pre.shape[0]
    x2 = x.reshape(n, D_MODEL)
    rope = None
    vec = lambda a: a.reshape(1, -1)
    for layer in range(depth):
        i = layer // 2
        g_pre = vec(norm_mix_pre[layer])
        if layer % 2 == 0:
            w_in = even_w_in[i]
            f0 = 3 * FOX_WIDTH
            w_pack = jnp.concatenate(
                [w_in[:, FOX_WIDTH:2 * FOX_WIDTH],
                 jnp.pad(w_in[:, f0:FOX_IN], ((0, 0), (0, LANES - N_FOX_HEADS))),
                 w_in[:, FOX_IN:]], axis=1).astype(BF16)
            w_qv_t = jnp.concatenate([w_in[:, :FOX_WIDTH], w_in[:, 2 * FOX_WIDTH:f0]],
                                     axis=1).T.astype(BF16)
            fb = jnp.pad(fox_forget_bias[i], (0, LANES - N_FOX_HEADS)).reshape(1, LANES)
            qt, kx, vt, ct, rz = _even_in(x2, g_pre, w_pack, w_qv_t, fb, batch)
            y_fox = _fox_attention(qt, kx, vt, ct, batch)
            w2p = _pad_rows(rwkv_w2[i], LANES).astype(BF16)
            a2p = _pad_rows(rwkv_a2[i], LANES, before=DECAY_LORA).astype(BF16)
            y_rwkv = _rwkv_mix(rz, vec(rwkv_mu[i]), vec(rwkv_w0[i]), w2p, vec(rwkv_a0[i]), a2p,
                               rwkv_g2[i].astype(BF16), vec(rwkv_k_k[i]), vec(rwkv_k_a[i]),
                               vec(rwkv_r_k[i]), vec(rwkv_ln_w[i]), vec(rwkv_ln_b[i]), batch)
            mixer_outs = [y_fox, y_rwkv]
            w_out = even_w_out[i]
        else:
            if rope is None:
                rope = _rope_tables(positions)
            qkv = _odd_in(x2, g_pre, odd_w_in[i].astype(BF16), *rope, batch)
            outs, lses = [], []
            for gi, (_, d) in enumerate(DILATED_GROUPS):
                q, k, v = [a.reshape(batch, d, -1, DIL_WIDTH) for a in qkv[3 * gi:3 * gi + 3]]
                o, lse = _dilated_branch(q, k, v, batch)
                outs.append(o)
                lses.append(lse)
            mixer_outs = outs + lses
            w_out = odd_w_out[i]
        x2 = _post_ffn(mixer_outs, x2, w_out.astype(BF16), vec(norm_mix_post[layer]),
                       vec(norm_ffn_pre[layer]), vec(norm_ffn_post[layer]),
                       ffn_w_up[layer].astype(BF16), _pad_rows(ffn_conv_w[layer], 8),
                       vec(ffn_conv_b[layer]), ffn_w_down[layer].astype(BF16), batch,
                       even=layer % 2 == 0)
    return x2.reshape(batch, seq, D_MODEL)
```

```python
import functools

import jax
import jax.numpy as jnp
from jax import lax
from jax.experimental import pallas as pl
from jax.experimental.pallas import tpu as pltpu

F32 = jnp.float32
BF16 = jnp.bfloat16

D_MODEL = 1024
HEAD_DIM = 64
N_FOX_HEADS = 8
FOX_WIDTH = 512
RWKV_WIDTH = 512
DECAY_LORA = 64
ICLR_LORA = 64
GATE_LORA = 128
RWKV_IN = 3 * RWKV_WIDTH + DECAY_LORA + ICLR_LORA + GATE_LORA
FOX_IN = 3 * FOX_WIDTH + N_FOX_HEADS
N_DIL_HEADS = 16
DIL_WIDTH = 1024
DILATED_GROUPS = ((128, 1), (512, 4), (2048, 16))
DIL_SPAN = 128
ROPE_THETA = 500000.0
ROPE_DIMS = 16
D_FF = 2816
RMS_EPS = 1e-6
GN_EPS = 64e-5
NEG_INF = -1e30
LOG2_E = 1.4426950408889634

LANES = 128
V7X_VMEM_LIMIT = 56 * 1024 * 1024

ROW_TILE = 512
FOX_TILE = 1024
RWKV_TILE = 512
RWKV_CHUNK = 64
RWKV_GROUP = 256
RWKV_SUM_TERMS = 1
DIL_QROWS = 1024
DIL_PAIRS_PER_STEP = 8
FFN_CHUNK = 256


def _dot(a, b):
    return jnp.dot(a, b, preferred_element_type=F32)


def _dot_nt(a, b):
    return lax.dot_general(a, b, (((1,), (1,)), ((), ())), preferred_element_type=F32)


def _dot_tn(a, b):
    return lax.dot_general(a, b, (((0,), (0,)), ((), ())), preferred_element_type=F32)


def _split(x, terms):
    parts = []
    for _ in range(terms):
        part = x.astype(BF16)
        parts.append(part)
        x = x - part.astype(F32)
    return parts


def _dot_exact_lhs(sel, x, terms=3):
    return functools.reduce(jnp.add, [_dot(sel, part) for part in _split(x, terms)])


def _dot_exact_rhs(x, sel, terms=3):
    return functools.reduce(jnp.add, [_dot(part, sel) for part in _split(x, terms)])


def _rms_norm(x, g):
    return x * lax.rsqrt(jnp.mean(x * x, axis=-1, keepdims=True) + RMS_EPS) * g


def _softplus(x):
    return jnp.maximum(x, 0.0) + jnp.log(1.0 + jnp.exp(-jnp.abs(x)))


def _sigmoid(x):
    return 1.0 / (1.0 + jnp.exp(-x))


def _const_spec(shape):
    nd = len(shape)
    return pl.BlockSpec(shape, lambda *_: (0,) * nd, pipeline_mode=pl.Buffered(1))


def _params(*sem):
    return pltpu.CompilerParams(dimension_semantics=sem, vmem_limit_bytes=V7X_VMEM_LIMIT)


EVEN_PACKED = FOX_WIDTH + LANES + RWKV_IN

FOX_PAIRS = FOX_WIDTH // LANES
FOX_KEY_LANES = 2 * LANES
FOX_BIAS_TERMS = 3


def _fox_key_bias_selectors():
    r = lax.broadcasted_iota(jnp.int32, (LANES, FOX_PAIRS * LANES), 0)
    cidx = lax.broadcasted_iota(jnp.int32, (LANES, FOX_PAIRS * LANES), 1)
    pair, slot = cidx // LANES, cidx % LANES
    return [jnp.where((r // 2 == pair) & (r < N_FOX_HEADS)
                      & (slot == FOX_BIAS_TERMS * (r % 2) + t), 1.0, 0.0).astype(BF16)
            for t in range(FOX_BIAS_TERMS)]


def _even_in_kernel(x_ref, g_ref, w_ref, wt_ref, fb_ref, qt_ref, kx_ref, vt_ref, ct_ref, rz_ref,
                    carry_ref, *, tiles_per_batch):
    i = pl.program_id(0)
    tm = x_ref.shape[0]
    h = _rms_norm(x_ref[...], g_ref[...]).astype(BF16)
    qt_ref[...] = (_dot_nt(wt_ref[0:FOX_WIDTH, :], h) * (LOG2_E * HEAD_DIM ** -0.5)).astype(BF16)
    vt_ref[...] = _dot_nt(wt_ref[FOX_WIDTH:, :], h).astype(BF16)
    f = _dot(h, w_ref[:, FOX_WIDTH:FOX_WIDTH + LANES]) + fb_ref[...]
    log2_f = -_softplus(-f) * LOG2_E

    @pl.when(i % tiles_per_batch == 0)
    def _():
        carry_ref[...] = jnp.zeros_like(carry_ref)

    row = lax.broadcasted_iota(jnp.int32, (tm, tm), 0)
    col = lax.broadcasted_iota(jnp.int32, (tm, tm), 1)
    tri = jnp.where(row >= col, 1.0, 0.0).astype(BF16)
    c = _dot_exact_lhs(tri, log2_f) + carry_ref[0:1, :]
    carry_ref[...] = jnp.broadcast_to(c[tm - 1:tm, :], carry_ref.shape)
    ct_ref[...] = jnp.transpose(c)[0:N_FOX_HEADS, :]

    k = _dot(h, w_ref[:, 0:FOX_WIDTH])
    bias = functools.reduce(jnp.add, [_dot(part, sel) for part, sel in
                                      zip(_split(-c, FOX_BIAS_TERMS), _fox_key_bias_selectors())])
    slot = lax.broadcasted_iota(jnp.int32, (tm, FOX_PAIRS * LANES), 1) % LANES
    ones = (slot >= 2 * FOX_BIAS_TERMS) & (slot < 3 * FOX_BIAS_TERMS)
    bias = jnp.where(ones, 1.0, bias).astype(BF16)
    for p in range(FOX_PAIRS):
        kx_ref[:, p * FOX_KEY_LANES:p * FOX_KEY_LANES + LANES] = (
            k[:, p * LANES:(p + 1) * LANES].astype(BF16))
        kx_ref[:, p * FOX_KEY_LANES + LANES:(p + 1) * FOX_KEY_LANES] = (
            bias[:, p * LANES:(p + 1) * LANES])
    rz_ref[...] = _dot(h, w_ref[:, FOX_WIDTH + LANES:])


def _even_in(x2, g, w_pack, w_qv_t, fb, batch):
    n = x2.shape[0]
    tm = ROW_TILE
    row = lambda w: pl.BlockSpec((tm, w), lambda i: (i, 0))
    col = lambda h: pl.BlockSpec((h, tm), lambda i: (0, i))
    return pl.pallas_call(
        functools.partial(_even_in_kernel, tiles_per_batch=n // batch // tm),
        grid=(n // tm,),
        in_specs=[row(D_MODEL), _const_spec((1, D_MODEL)), _const_spec((D_MODEL, EVEN_PACKED)),
                  _const_spec((2 * FOX_WIDTH, D_MODEL)), _const_spec((1, LANES))],
        out_specs=[col(FOX_WIDTH), row(FOX_PAIRS * FOX_KEY_LANES), col(FOX_WIDTH),
                   col(N_FOX_HEADS), row(RWKV_IN)],
        out_shape=[jax.ShapeDtypeStruct((FOX_WIDTH, n), BF16),
                   jax.ShapeDtypeStruct((n, FOX_PAIRS * FOX_KEY_LANES), BF16),
                   jax.ShapeDtypeStruct((FOX_WIDTH, n), BF16),
                   jax.ShapeDtypeStruct((N_FOX_HEADS, n), F32),
                   jax.ShapeDtypeStruct((n, RWKV_IN), F32)],
        scratch_shapes=[pltpu.VMEM((8, LANES), F32)],
        compiler_params=_params("arbitrary"),
        name="even_in",
    )(x2, g, w_pack, w_qv_t, fb)


def _fox_kernel(qt_ref, kx_ref, vt_ref, ct_ref, o_ref, s_ref):
    hp = pl.program_id(1)
    qi = pl.program_id(2)
    tq = qt_ref.shape[1]
    qt = qt_ref[...]
    row = lax.broadcasted_iota(jnp.int32, (LANES, tq), 0)
    head0 = row < HEAD_DIM
    zero = jnp.zeros_like(qt)

    def bias_rows(hh):
        c = ct_ref[pl.ds(2 * hp + hh, 1), :]
        terms = [part.astype(F32) for part in _split(c, FOX_BIAS_TERMS)]
        b = jnp.zeros((LANES, tq), F32)
        for t, term in enumerate(terms):
            b = jnp.where(row == 2 * FOX_BIAS_TERMS + t, term, b)
        own = (row >= FOX_BIAS_TERMS * hh) & (row < FOX_BIAS_TERMS * (hh + 1))
        return jnp.where(own, 1.0, b).astype(BF16)

    qxt = jnp.concatenate([
        jnp.concatenate([jnp.where(head0, qt, zero), bias_rows(0)], axis=0),
        jnp.concatenate([jnp.where(head0, zero, qt), bias_rows(1)], axis=0)], axis=1)
    tk = tq
    key_idx = lax.broadcasted_iota(jnp.int32, (tk, 2 * tq), 0)
    query_idx = lax.broadcasted_iota(jnp.int32, (tk, 2 * tq), 1) % tq
    causal = key_idx <= query_idx

    def step(j, carry, masked):
        m, l, acc = carry
        k0 = pl.multiple_of(j * tk, tk)
        s = _dot(kx_ref[pl.ds(k0, tk), :], qxt)
        if masked:
            s = jnp.where(causal, s, NEG_INF)
        s_ref[...] = s
        m_new = jnp.maximum(m, jnp.max(s_ref[...], axis=0, keepdims=True))
        p = jnp.exp2(s_ref[...] - m_new)
        alpha = jnp.exp2(m - m_new)
        l = alpha * l + jnp.sum(p, axis=0, keepdims=True)
        acc = alpha * acc + _dot(vt_ref[:, pl.ds(k0, tk)], p.astype(BF16))
        return m_new, l, acc

    init = (jnp.full((1, 2 * tq), NEG_INF, F32), jnp.zeros((1, 2 * tq), F32),
            jnp.zeros((LANES, 2 * tq), F32))
    carry = lax.fori_loop(0, qi, lambda j, c: step(j, c, False), init)
    _, l, acc = step(qi, carry, True)
    out = acc / l
    o_ref[...] = jnp.where(head0, out[:, :tq], out[:, tq:]).astype(BF16)


def _fox_attention(qt, kx, vt, ct, batch):
    n = qt.shape[1]
    t = n // batch
    tq = FOX_TILE
    nq = t // tq
    return pl.pallas_call(
        _fox_kernel,
        grid=(batch, FOX_PAIRS, nq),
        in_specs=[
            pl.BlockSpec((LANES, tq), lambda b, p, i: (p, b * nq + i)),
            pl.BlockSpec((t, FOX_KEY_LANES), lambda b, p, i: (b, p)),
            pl.BlockSpec((LANES, t), lambda b, p, i: (p, b)),
            pl.BlockSpec((N_FOX_HEADS, tq), lambda b, p, i: (0, b * nq + i)),
        ],
        out_specs=pl.BlockSpec((LANES, tq), lambda b, p, i: (p, b * nq + i)),
        out_shape=jax.ShapeDtypeStruct((FOX_WIDTH, n), BF16),
        scratch_shapes=[pltpu.VMEM((tq, 2 * tq), F32)],
        compiler_params=_params("arbitrary", "arbitrary", "arbitrary"),
        name="fox_attention",
    )(qt, kx, vt, ct)


def _rwkv_kernel(rz_ref, mu_ref, w0_ref, w2_ref, a0_ref, a2_ref, g2_ref, kk_ref, ka_ref, rk_ref,
                 lnw_ref, lnb_ref, y_ref, state_ref, tail_ref, ybuf_ref):
    tc = rz_ref.shape[0]
    ch = RWKV_CHUNK
    gw = RWKV_GROUP
    n_groups = RWKV_WIDTH // gw

    @pl.when(pl.program_id(1) == 0)
    def _():
        state_ref[...] = jnp.zeros_like(state_ref)
        tail_ref[...] = jnp.zeros_like(tail_ref)

    z = rz_ref[...]
    rows = lax.broadcasted_iota(jnp.int32, z.shape, 0)
    z_prev = jnp.where(rows == 0, tail_ref[7:8, :], pltpu.roll(z, 1, axis=0))
    tail_ref[...] = z[tc - 8:tc, :]
    z = z + (z_prev - z) * mu_ref[...]
    r = z[:, 0:RWKV_WIDTH]
    k = z[:, RWKV_WIDTH:2 * RWKV_WIDTH]
    v = z[:, 2 * RWKV_WIDTH:3 * RWKV_WIDTH]
    lo = z[:, 3 * RWKV_WIDTH:3 * RWKV_WIDTH + LANES]
    g_lo = z[:, 3 * RWKV_WIDTH + LANES:]

    w = w0_ref[...] + _dot(jnp.tanh(lo).astype(BF16), w2_ref[...])
    w = -_softplus(-w) - 0.5
    log2_decay = -jnp.exp(w) * LOG2_E
    a = _sigmoid(a0_ref[...] + _dot(lo.astype(BF16), a2_ref[...]))
    gate = _dot(_sigmoid(g_lo).astype(BF16), g2_ref[...])

    br = lax.broadcasted_iota(jnp.int32, (gw, gw), 0) // HEAD_DIM
    bc = lax.broadcasted_iota(jnp.int32, (gw, gw), 1) // HEAD_DIM
    same_head = br == bc
    head_ones = jnp.where(same_head, 1.0, 0.0).astype(BF16)

    def seg_sum(t):
        return jnp.concatenate(
            [_dot_exact_rhs(t[:, g * gw:(g + 1) * gw], head_ones, terms=RWKV_SUM_TERMS)
             for g in range(n_groups)], axis=1)

    kk = k * kk_ref[...]
    kk = kk / jnp.maximum(jnp.sqrt(seg_sum(kk * kk)), 1e-12)
    k = k * (1.0 + (a - 1.0) * ka_ref[...])
    aa = -kk
    bb = kk * a
    bonus = seg_sum(r * k * rk_ref[...]) * v

    tr = lax.broadcasted_iota(jnp.int32, (ch, ch), 0)
    ts = lax.broadcasted_iota(jnp.int32, (ch, ch), 1)
    tri_incl = jnp.where(tr >= ts, 1.0, 0.0).astype(BF16)
    gr = lax.broadcasted_iota(jnp.int32, (ch, gw), 0)
    gs = lax.broadcasted_iota(jnp.int32, (ch, gw), 1) % ch
    lower = gr >= gs
    strict = gr > gs
    eye = jnp.where(gr == gs, 1.0, 0.0)

    def block_diag(t):
        tiled = jnp.concatenate([t] * (gw // ch), axis=0)
        return jnp.where(same_head, tiled, 0.0).astype(BF16)

    stack = lambda x, y: jnp.concatenate([x, y], axis=0).astype(BF16)
    n_chunks = tc // ch
    chains = [(c, grp) for c in range(n_chunks) for grp in range(n_groups)]
    cs = {}
    for c in range(n_chunks):
        sl = slice(c * ch, (c + 1) * ch)
        ld = log2_decay[sl]
        cum = _dot_exact_lhs(tri_incl, ld)
        cum_end = cum[ch - 1:ch, :]
        p_inv = jnp.exp2(-cum)
        p_rest = jnp.exp2(cum_end - cum)
        cs[c] = dict(p_end=jnp.exp2(cum_end), a_t=aa[sl] * jnp.exp2(cum - ld),
                     r_t=r[sl] * jnp.exp2(cum), b_t=bb[sl] * p_inv, k_t=k[sl] * p_inv,
                     b_h=bb[sl] * p_rest, k_h=k[sl] * p_rest, v=v[sl])
    st = {}
    for c, grp in chains:
        gl = slice(grp * gw, (grp + 1) * gw)
        d = {name: val[:, gl] for name, val in cs[c].items()}
        ar = stack(d["a_t"], d["r_t"])
        sb = _dot_nt(ar, block_diag(d["b_t"]))
        sk = _dot_nt(ar, block_diag(d["k_t"]))
        d["a_ab"] = jnp.where(strict, sb[:ch], 0.0)
        d["a_rb"] = jnp.where(lower, sb[ch:], 0.0)
        a_ak = jnp.where(strict, sk[:ch], 0.0)
        a_rk = jnp.where(lower, sk[ch:], 0.0)
        av = _dot(stack(a_ak, a_rk), block_diag(d["v"]))
        d["av"], d["rv"] = av[:ch], av[ch:]
        d["inv"] = eye + d["a_ab"]
        d["power"] = d["a_ab"]
        st[c, grp] = d
    levels = ch.bit_length() - 1
    for lvl in range(levels):
        for key in chains:
            d = st[key]
            rhs = block_diag(d["power"])
            if lvl == 0:
                d["power"] = _dot(d["power"].astype(BF16), rhs)
            elif lvl < levels - 1:
                both = _dot(stack(d["power"], d["inv"]), rhs)
                d["power"] = both[:ch]
                d["inv"] = d["inv"] + both[ch:]
            else:
                d["inv"] = d["inv"] + _dot(d["inv"].astype(BF16), rhs)
    for key in chains:
        d = st[key]
        inv = d["inv"].astype(BF16)
        d["ta"] = _dot(inv, block_diag(d["a_t"]))
        d["u0"] = _dot(inv, block_diag(d["av"]))
    for key in chains:
        d = st[key]
        a_rb = d["a_rb"].astype(BF16)
        d["query"] = d["r_t"] + _dot(a_rb, block_diag(d["ta"]))
        d["y0"] = d["rv"] + _dot(a_rb, block_diag(d["u0"]))
        d["mix"] = jnp.where(same_head, _dot_tn(d["ta"].astype(BF16), d["b_h"].astype(BF16)), 0.0)
        d["add"] = jnp.where(same_head, _dot_tn(stack(d["u0"], d["v"]),
                                                stack(d["b_h"], d["k_h"])), 0.0)
    for grp in range(n_groups):
        gl = slice(grp * gw, (grp + 1) * gw)
        g_state = state_ref[grp]
        for c in range(n_chunks):
            d = st[c, grp]
            g_bf = g_state.astype(BF16)
            ybuf_ref[c * ch:(c + 1) * ch, gl] = _dot_nt(d["query"].astype(BF16), g_bf) + d["y0"]
            g_state = g_state * d["p_end"] + _dot(g_bf, d["mix"].astype(BF16)) + d["add"]
        state_ref[grp] = g_state

    y = ybuf_ref[...]
    inv_n = 1.0 / HEAD_DIM
    mean = seg_sum(y) * inv_n
    d = y - mean
    var = seg_sum(d * d) * inv_n
    yn = d * lax.rsqrt(var + GN_EPS) * lnw_ref[...] + lnb_ref[...]
    y_ref[...] = ((yn + bonus) * gate).astype(BF16)


def _rwkv_mix(rz, mu, w0, w2p, a0, a2p, g2, k_k, k_a, r_k, ln_w, ln_b, batch):
    n = rz.shape[0]
    t = n // batch
    tc = RWKV_TILE
    nt = t // tc
    vec = lambda w: _const_spec((1, w))
    return pl.pallas_call(
        _rwkv_kernel,
        grid=(batch, nt),
        in_specs=[pl.BlockSpec((tc, RWKV_IN), lambda b, i: (b * nt + i, 0)),
                  vec(RWKV_IN), vec(RWKV_WIDTH), _const_spec((LANES, RWKV_WIDTH)),
                  vec(RWKV_WIDTH), _const_spec((LANES, RWKV_WIDTH)),
                  _const_spec((GATE_LORA, RWKV_WIDTH)),
                  vec(RWKV_WIDTH), vec(RWKV_WIDTH), vec(RWKV_WIDTH), vec(RWKV_WIDTH),
                  vec(RWKV_WIDTH)],
        out_specs=pl.BlockSpec((tc, RWKV_WIDTH), lambda b, i: (b * nt + i, 0)),
        out_shape=jax.ShapeDtypeStruct((n, RWKV_WIDTH), BF16),
        scratch_shapes=[pltpu.VMEM((RWKV_WIDTH // RWKV_GROUP, RWKV_GROUP, RWKV_GROUP), F32),
                        pltpu.VMEM((8, RWKV_IN), F32),
                        pltpu.VMEM((tc, RWKV_WIDTH), F32)],
        compiler_params=_params("arbitrary", "arbitrary"),
        name="rwkv_mix",
    )(rz, mu, w0, w2p, a0, a2p, g2, k_k, k_a, r_k, ln_w, ln_b)


def _rope_table_kernel(pos_ref, freq_ref, cos_ref, sin_ref):
    ang = pos_ref[...] * freq_ref[...]
    s = jnp.sin(ang)
    d = lax.broadcasted_iota(jnp.int32, ang.shape, 1) % HEAD_DIM
    half = ROPE_DIMS // 2
    cos_ref[...] = jnp.where(d < ROPE_DIMS, jnp.cos(ang), 1.0)
    sin_ref[...] = jnp.where(d < half, -s, jnp.where(d < ROPE_DIMS, s, 0.0))


def _rope_tables(positions):
    n = positions.size
    tm = 2048
    pos = jnp.broadcast_to(positions.reshape(n, 1).astype(F32), (n, LANES))
    half = ROPE_DIMS // 2
    inv_freq = ROPE_THETA ** (-jnp.arange(0, ROPE_DIMS, 2, dtype=F32) / ROPE_DIMS)
    d = jnp.arange(LANES) % HEAD_DIM
    freq = jnp.where(d < ROPE_DIMS, inv_freq[d % half], 0.0).reshape(1, LANES)
    row = pl.BlockSpec((tm, LANES), lambda i: (i, 0))
    return pl.pallas_call(
        _rope_table_kernel,
        grid=(n // tm,),
        in_specs=[row, _const_spec((1, LANES))],
        out_specs=[row, row],
        out_shape=[jax.ShapeDtypeStruct((n, LANES), F32)] * 2,
        compiler_params=_params("arbitrary"),
        name="rope_tables",
    )(pos, freq)


def _odd_in_kernel(x_ref, g_ref, w_ref, cos_ref, sin_ref, *refs):
    n_perm = len(DILATED_GROUPS) - 1
    outs, perm_refs = refs[:-n_perm], refs[-n_perm:]
    tm = x_ref.shape[0]
    h = _rms_norm(x_ref[...], g_ref[...]).astype(BF16)
    cw = 2 * LANES
    cos = jnp.concatenate([cos_ref[...]] * (cw // LANES), axis=1)
    sin = jnp.concatenate([sin_ref[...]] * (cw // LANES), axis=1)
    half = ROPE_DIMS // 2
    first_half = lax.broadcasted_iota(jnp.int32, (tm, cw), 1) % HEAD_DIM < half

    def rotary(t):
        partner = jnp.where(first_half, pltpu.roll(t, cw - half, axis=1),
                            pltpu.roll(t, half, axis=1))
        return t * cos + partner * sin

    for idx in range(3):
        for c in range(DIL_WIDTH // cw):
            col = idx * DIL_WIDTH + c * cw
            z = _dot(h, w_ref[:, col:col + cw])
            if idx < 2:
                z = rotary(z)
            if idx == 0:
                z = z * (LOG2_E * HEAD_DIM ** -0.5)
            outs[idx][:, c * cw:(c + 1) * cw] = z.astype(BF16)
            for j in range(cw // LANES):
                blk = c * (cw // LANES) + j
                perm_refs[0][blk] = z[:, j * LANES:(j + 1) * LANES]
                for gi in range(1, len(DILATED_GROUPS)):
                    d_prev, d = DILATED_GROUPS[gi - 1][1], DILATED_GROUPS[gi][1]
                    ref, src_ref = outs[gi * 3 + idx], perm_refs[gi - 1]
                    rows = tm // d
                    for rho in range(d):
                        start = (rho % d_prev) * (tm // d_prev) + rho // d_prev
                        part = src_ref[blk, pl.ds(start, rows, stride=d // d_prev), :]
                        ref[0, rho, :, blk * LANES:(blk + 1) * LANES] = part.astype(BF16)
                        if gi + 1 < len(DILATED_GROUPS):
                            perm_refs[gi][blk, rho * rows:(rho + 1) * rows, :] = part


def _residue_spec(d, tm, width, tiles_per_batch):
    return pl.BlockSpec((1, d, tm // d, width),
                        lambda i: (i // tiles_per_batch, 0, i % tiles_per_batch, 0))


def _odd_in(x2, g, w, cos, sin, batch):
    n = x2.shape[0]
    t = n // batch
    tm = ROW_TILE
    row = lambda w_: pl.BlockSpec((tm, w_), lambda i: (i, 0))
    out_specs = [row(DIL_WIDTH)] * 3
    out_shape = [jax.ShapeDtypeStruct((n, DIL_WIDTH), BF16)] * 3
    for _, d in DILATED_GROUPS[1:]:
        out_specs += [_residue_spec(d, tm, DIL_WIDTH, t // tm)] * 3
        out_shape += [jax.ShapeDtypeStruct((batch, d, t // d, DIL_WIDTH), BF16)] * 3
    return pl.pallas_call(
        _odd_in_kernel,
        grid=(n // tm,),
        in_specs=[row(D_MODEL), _const_spec((1, D_MODEL)), _const_spec((D_MODEL, 3 * DIL_WIDTH)),
                  row(LANES), row(LANES)],
        out_specs=out_specs,
        out_shape=out_shape,
        scratch_shapes=[pltpu.VMEM((DIL_WIDTH // LANES, tm, LANES), F32)] * (len(DILATED_GROUPS) - 1),
        compiler_params=_params("arbitrary"),
        name="odd_in",
    )(x2, g, w, cos, sin)


def _dilated_kernel(q_ref, kc_ref, kp_ref, vc_ref, vp_ref, o_ref, lse_ref):
    blk = pl.program_id(2)
    sp = DIL_SPAN
    qrows = q_ref.shape[2]
    lane = lax.broadcasted_iota(jnp.int32, (sp, LANES), 1)
    head0 = lane < HEAD_DIM
    qi = lax.broadcasted_iota(jnp.int32, (2 * sp, 2 * sp), 0) % sp
    ki = lax.broadcasted_iota(jnp.int32, (2 * sp, 2 * sp), 1)
    band = (ki >= qi) & (ki <= qi + sp)
    bias = jnp.where(band, 0.0, NEG_INF)
    bias_first = jnp.where(band & (ki >= jnp.where(blk > 0, 0, sp)), 0.0, NEG_INF)
    n_steps = DIL_WIDTH // LANES // DIL_PAIRS_PER_STEP

    for sub in range(qrows // sp):
        r0 = sub * sp

        def step(g, lse_acc, r0=r0, sub=sub):
            for pp in range(DIL_PAIRS_PER_STEP):
                p = g * DIL_PAIRS_PER_STEP + pp
                cols = pl.ds(pl.multiple_of(p * LANES, LANES), LANES)
                q = q_ref[0, 0, r0:r0 + sp, cols]
                if sub == 0:
                    kb = jnp.concatenate([kp_ref[0, 0, qrows - sp:qrows, cols],
                                          kc_ref[0, 0, 0:sp, cols]], axis=0)
                    vb = jnp.concatenate([vp_ref[0, 0, qrows - sp:qrows, cols],
                                          vc_ref[0, 0, 0:sp, cols]], axis=0)
                else:
                    kb = kc_ref[0, 0, r0 - sp:r0 + sp, cols]
                    vb = vc_ref[0, 0, r0 - sp:r0 + sp, cols]
                zero = jnp.zeros_like(q)
                q2 = jnp.concatenate([jnp.where(head0, q, zero), jnp.where(head0, zero, q)], axis=0)
                s = _dot_nt(q2, kb) + (bias_first if sub == 0 else bias)
                m = jnp.max(s, axis=1, keepdims=True)
                e = jnp.exp2(s - m)
                den = jnp.sum(e, axis=1, keepdims=True)
                o2 = _dot(e.astype(BF16), vb) * (1.0 / den)
                lse = m + jnp.log2(den)
                o_ref[0, 0, r0:r0 + sp, cols] = jnp.where(head0, o2[:sp], o2[sp:]).astype(o_ref.dtype)
                lse_acc = jnp.where(lane == 2 * p, lse[:sp],
                                    jnp.where(lane == 2 * p + 1, lse[sp:], lse_acc))
            return lse_acc

        lse_all = lax.fori_loop(0, n_steps, step, jnp.zeros((sp, LANES), F32))
        lse_ref[0, 0, r0:r0 + sp, :] = lse_all


def _dilated_branch(q, k, v, batch):
    _, dilation, length, _ = q.shape
    qrows = min(DIL_QROWS, length)
    nb = length // qrows
    cur = pl.BlockSpec((1, 1, qrows, DIL_WIDTH), lambda b, r, i: (b, r, i, 0))
    prev = pl.BlockSpec((1, 1, qrows, DIL_WIDTH), lambda b, r, i: (b, r, jnp.maximum(i - 1, 0), 0))
    return pl.pallas_call(
        _dilated_kernel,
        grid=(batch, dilation, nb),
        in_specs=[cur, cur, prev, cur, prev],
        out_specs=[cur, pl.BlockSpec((1, 1, qrows, LANES), lambda b, r, i: (b, r, i, 0))],
        out_shape=[jax.ShapeDtypeStruct((batch, dilation, length, DIL_WIDTH), BF16),
                   jax.ShapeDtypeStruct((batch, dilation, length, LANES), F32)],
        compiler_params=_params("arbitrary", "arbitrary", "arbitrary"),
        name=f"dilated_d{dilation}",
    )(q, k, k, v, v)


def _gelu_tanh(x):
    return 0.5 * x * (1.0 + jnp.tanh(0.7978845608028654 * (x + 0.044715 * x * x * x)))


def _finish_layer(m, x_ref, gmp_ref, gfp_ref, gfo_ref, wup_ref, cw_ref, cb_ref, wdn_ref, out_ref,
                  tail_ref, act_ref, tiles_per_batch):
    i = pl.program_id(0)
    tm = x_ref.shape[0]
    x1 = x_ref[...] + _rms_norm(m, gmp_ref[...])
    h = _rms_norm(x1, gfp_ref[...]).astype(BF16)

    @pl.when(i % tiles_per_batch == 0)
    def _():
        tail_ref[...] = jnp.zeros_like(tail_ref)

    ck = FFN_CHUNK
    rows = lax.broadcasted_iota(jnp.int32, (tm, ck), 0)

    def conv(col):
        u = _dot(h, wup_ref[:, col:col + ck])
        t1 = tail_ref[7:8, col:col + ck]
        t2 = tail_ref[6:7, col:col + ck]
        u1 = jnp.where(rows == 0, t1, pltpu.roll(u, 1, axis=0))
        u2 = jnp.where(rows == 0, t2, jnp.where(rows == 1, t1, pltpu.roll(u, 2, axis=0)))
        tail_ref[:, col:col + ck] = u[tm - 8:tm, :]
        return (cb_ref[:, col:col + ck] + cw_ref[2:3, col:col + ck] * u
                + cw_ref[1:2, col:col + ck] * u1 + cw_ref[0:1, col:col + ck] * u2)

    for c in range(D_FF // ck):
        gate = conv(c * ck)
        val = conv(D_FF + c * ck)
        act_ref[:, c * ck:(c + 1) * ck] = (_gelu_tanh(gate) * val).astype(BF16)
    f = _dot(act_ref[...], wdn_ref[...])
    out_ref[...] = x1 + _rms_norm(f, gfo_ref[...])


def _even_post_kernel(yft_ref, yr_ref, x_ref, wo_ref, *rest, tiles_per_batch):
    m = _dot_tn(yft_ref[...], wo_ref[0:FOX_WIDTH, :]) + _dot(yr_ref[...], wo_ref[FOX_WIDTH:, :])
    _finish_layer(m, x_ref, *rest, tiles_per_batch=tiles_per_batch)


def _time_order(src_ref, perm_ref):
    _, d, rows, width = src_ref.shape
    if d == 1:
        return src_ref[0, 0].astype(F32)
    blocks = width // LANES
    for rho in range(d):
        for c in range(blocks):
            perm_ref[c, pl.ds(rho, rows, stride=d), :] = src_ref[
                0, rho, :, c * LANES:(c + 1) * LANES].astype(F32)
    return jnp.concatenate([perm_ref[c] for c in range(blocks)], axis=1)


def _odd_post_kernel(*refs, tiles_per_batch):
    ng = len(DILATED_GROUPS)
    o_refs, l_refs = refs[:ng], refs[ng:2 * ng]
    x_ref, wo_ref = refs[2 * ng:2 * ng + 2]
    rest, (perm_o_ref, perm_l_ref) = refs[2 * ng + 2:-2], refs[-2:]
    lses = [_time_order(l_ref, perm_l_ref) for l_ref in l_refs]
    m = functools.reduce(jnp.maximum, lses)
    es = [jnp.exp2(l - m) for l in lses]
    inv = 1.0 / functools.reduce(jnp.add, es)
    hr = lax.broadcasted_iota(jnp.int32, (LANES, DIL_WIDTH), 0)
    hc = lax.broadcasted_iota(jnp.int32, (LANES, DIL_WIDTH), 1) // HEAD_DIM
    expand = jnp.where(hr == hc, 1.0, 0.0).astype(BF16)
    o = None
    for e, o_ref in zip(es, o_refs):
        term = _dot_exact_rhs(e * inv, expand, terms=1) * _time_order(o_ref, perm_o_ref)
        o = term if o is None else o + term
    _finish_layer(_dot(o.astype(BF16), wo_ref[...]), x_ref, *rest, tiles_per_batch=tiles_per_batch)


def _post_ffn(mixer_outs, x2, wo, g_mix_post, g_ffn_pre, g_ffn_post, w_up, conv_w, conv_b, w_dn,
              batch, even):
    n = x2.shape[0]
    tm = ROW_TILE
    tpb = n // batch // tm
    row = lambda w: pl.BlockSpec((tm, w), lambda i: (i, 0))
    vec = _const_spec((1, D_MODEL))
    def spec(a):
        if a.ndim == 4:
            return _residue_spec(a.shape[1], tm, a.shape[3], tpb)
        if a.shape[0] == n:
            return row(a.shape[1])
        return pl.BlockSpec((a.shape[0], tm), lambda i: (0, i))

    mixer_specs = [spec(a) for a in mixer_outs]
    body = _even_post_kernel if even else _odd_post_kernel
    scratch = [pltpu.VMEM((8, 2 * D_FF), F32), pltpu.VMEM((tm, D_FF), BF16)]
    if not even:
        scratch += [pltpu.VMEM((DIL_WIDTH // LANES, tm, LANES), F32), pltpu.VMEM((1, tm, LANES), F32)]
    return pl.pallas_call(
        functools.partial(body, tiles_per_batch=tpb),
        grid=(n // tm,),
        in_specs=mixer_specs + [row(D_MODEL), _const_spec((D_MODEL, D_MODEL)), vec, vec, vec,
                                _const_spec((D_MODEL, 2 * D_FF)), _const_spec((8, 2 * D_FF)),
                                _const_spec((1, 2 * D_FF)), _const_spec((D_FF, D_MODEL))],
        out_specs=row(D_MODEL),
        out_shape=jax.ShapeDtypeStruct((n, D_MODEL), F32),
        scratch_shapes=scratch,
        compiler_params=_params("arbitrary"),
        name="post_ffn_even" if even else "post_ffn_odd",
    )(*mixer_outs, x2, wo, g_mix_post, g_ffn_pre, g_ffn_post, w_up, conv_w, conv_b, w_dn)


def _pad_rows(a, rows, before=0):
    return jnp.pad(a, ((before, rows - a.shape[0] - before), (0, 0)))


def kernel(x, positions, norm_mix_pre, norm_mix_post, norm_ffn_pre, norm_ffn_post, even_w_in, fox_forget_bias, rwkv_mu, rwkv_w0, rwkv_w2, rwkv_a0, rwkv_a2, rwkv_g2, rwkv_k_k, rwkv_k_a, rwkv_r_k, rwkv_ln_w, rwkv_ln_b, even_w_out, odd_w_in, odd_w_out, ffn_w_up, ffn_conv_w, ffn_conv_b, ffn_w_down):
    batch, seq, _ = x.shape
    n = batch * seq
    depth = norm_mix_pre.shape[0]
    x2 = x.reshape(n, D_MODEL)
    rope = None
    vec = lambda a: a.reshape(1, -1)
    for layer in range(depth):
        i = layer // 2
        g_pre = vec(norm_mix_pre[layer])
        if layer % 2 == 0:
            w_in = even_w_in[i]
            f0 = 3 * FOX_WIDTH
            w_pack = jnp.concatenate(
                [w_in[:, FOX_WIDTH:2 * FOX_WIDTH],
                 jnp.pad(w_in[:, f0:FOX_IN], ((0, 0), (0, LANES - N_FOX_HEADS))),
                 w_in[:, FOX_IN:]], axis=1).astype(BF16)
            w_qv_t = jnp.concatenate([w_in[:, :FOX_WIDTH], w_in[:, 2 * FOX_WIDTH:f0]],
                                     axis=1).T.astype(BF16)
            fb = jnp.pad(fox_forget_bias[i], (0, LANES - N_FOX_HEADS)).reshape(1, LANES)
            qt, kx, vt, ct, rz = _even_in(x2, g_pre, w_pack, w_qv_t, fb, batch)
            y_fox = _fox_attention(qt, kx, vt, ct, batch)
            w2p = _pad_rows(rwkv_w2[i], LANES).astype(BF16)
            a2p = _pad_rows(rwkv_a2[i], LANES, before=DECAY_LORA).astype(BF16)
            y_rwkv = _rwkv_mix(rz, vec(rwkv_mu[i]), vec(rwkv_w0[i]), w2p, vec(rwkv_a0[i]), a2p,
                               rwkv_g2[i].astype(BF16), vec(rwkv_k_k[i]), vec(rwkv_k_a[i]),
                               vec(rwkv_r_k[i]), vec(rwkv_ln_w[i]), vec(rwkv_ln_b[i]), batch)
            mixer_outs = [y_fox, y_rwkv]
            w_out = even_w_out[i]
        else:
            if rope is None:
                rope = _rope_tables(positions)
            qkv = _odd_in(x2, g_pre, odd_w_in[i].astype(BF16), *rope, batch)
            outs, lses = [], []
            for gi, (_, d) in enumerate(DILATED_GROUPS):
                q, k, v = [a.reshape(batch, d, -1, DIL_WIDTH) for a in qkv[3 * gi:3 * gi + 3]]
                o, lse = _dilated_branch(q, k, v, batch)
                outs.append(o)
                lses.append(lse)
            mixer_outs = outs + lses
            w_out = odd_w_out[i]
        x2 = _post_ffn(mixer_outs, x2, w_out.astype(BF16), vec(norm_mix_post[layer]),
                       vec(norm_ffn_pre[layer]), vec(norm_ffn_post[layer]),
                       ffn_w_up[layer].astype(BF16), _pad_rows(ffn_conv_w[layer], 8),
                       vec(ffn_conv_b[layer]), ffn_w_down[layer].astype(BF16), batch,
                       even=layer % 2 == 0)
    return x2.reshape(batch, seq, D_MODEL)
```
